```python
import jax, jax.numpy as jnp
from jax import lax
import numpy as np

D_MODEL = 4096
BATCH = 4
SEQ = 2048
DEPTH = 1
DEC_BATCH = 128
DEC_SEQ = 4
PAST_LEN = 16384
PAGE_SIZE = 128

N_MEM = 256
D_POOL = 3 * D_MODEL // 8
D_CONV = 3 * D_MODEL // 8
D_XATTN = D_MODEL - D_POOL - D_CONV
N_XHEADS = 4
XHEAD_DIM = D_XATTN // N_XHEADS
POOL_WINDOWS = (2, 4, 8, 16)
N_POOL_GROUPS = len(POOL_WINDOWS)
POOL_GROUP = D_POOL // N_POOL_GROUPS
POOL_STATE = max(POOL_WINDOWS) - 1
CONV_WIDTH = 31
CONV_STATE = CONV_WIDTH - 1
D_IN = 2 * D_POOL + 3 * D_CONV + 2 * D_XATTN
EPS = 1e-6

kernel_name = "pool_conv_memory_hybrid_step"


def rmsnorm(x, g):
    xf = x.astype(jnp.float32)
    y = xf * lax.rsqrt(jnp.mean(xf * xf, axis=-1, keepdims=True) + EPS) * g.astype(jnp.float32)
    return y.astype(x.dtype)


def layernorm(x, g, b):
    xf = x.astype(jnp.float32)
    mu = jnp.mean(xf, axis=-1, keepdims=True)
    var = jnp.mean(jnp.square(xf - mu), axis=-1, keepdims=True)
    y = (xf - mu) * lax.rsqrt(var + EPS) * g.astype(jnp.float32) + b.astype(jnp.float32)
    return y.astype(x.dtype)


def memory_kv(mem, mem_norm_g, w_mem_k, w_mem_v):
    h = rmsnorm(mem, mem_norm_g)
    b = mem.shape[0]
    k = (h @ w_mem_k).reshape(b, N_MEM, N_XHEADS, XHEAD_DIM)
    v = (h @ w_mem_v).reshape(b, N_MEM, N_XHEADS, XHEAD_DIM)
    return k, v


def pool_mix(u, state, start_pos, w_pool, pool_scale):
    b, t, _ = u.shape
    ext = jnp.concatenate([state.astype(u.dtype), u], axis=1).astype(jnp.float32)
    cs = jnp.pad(jnp.cumsum(ext, axis=1), ((0, 0), (1, 0), (0, 0)))
    end = cs[:, POOL_STATE + 1:POOL_STATE + 1 + t]
    pos = start_pos + jnp.arange(t, dtype=jnp.int32)
    means = []
    for gi, w in enumerate(POOL_WINDOWS):
        lo_c, hi_c = gi * POOL_GROUP, (gi + 1) * POOL_GROUP
        lo = cs[:, POOL_STATE + 1 - w:POOL_STATE + 1 - w + t, lo_c:hi_c]
        cnt = jnp.minimum(w, pos + 1).astype(jnp.float32)[None, :, None]
        means.append((end[..., lo_c:hi_c] - lo) / cnt)
    pooled = jnp.concatenate(means, axis=-1) - ext[:, POOL_STATE:]
    mixed = jnp.einsum('btgc,gcd->btgd', pooled.reshape(b, t, N_POOL_GROUPS, POOL_GROUP),
                       w_pool.astype(jnp.float32)).reshape(b, t, D_POOL)
    mixed = mixed * pool_scale.astype(jnp.float32)
    return mixed.astype(u.dtype), ext[:, -POOL_STATE:].astype(u.dtype)


def conv_module(a, state, w_dw, b_dw, ln_g, ln_b, w_pw):
    ext = jnp.concatenate([state.astype(a.dtype), a], axis=1)
    y = lax.conv_general_dilated(ext, w_dw.astype(a.dtype)[:, None, :], window_strides=(1,), padding='VALID',
                                 dimension_numbers=('NWC', 'WIO', 'NWC'), feature_group_count=D_CONV)
    y = y + b_dw
    y = layernorm(y, ln_g, ln_b)
    y = jax.nn.silu(y) @ w_pw
    return y, ext[:, -CONV_STATE:]


def memory_attention(q, k, v):
    s = jnp.einsum('bthe,bmhe->bhtm', q.astype(jnp.float32), k.astype(jnp.float32)) * (XHEAD_DIM ** -0.5)
    p = jax.nn.softmax(s, axis=-1)
    o = jnp.einsum('bhtm,bmhe->bthe', p, v.astype(jnp.float32))
    b, t = q.shape[0], q.shape[1]
    return o.reshape(b, t, D_XATTN).astype(q.dtype)


def hybrid_layer(x, mem_k, mem_v, pool_state, conv_state, start_pos, norm_g, w_in, w_pool, pool_scale,
                 w_dw, b_dw, conv_ln_g, conv_ln_b, w_pw, w_out):
    b, t, _ = x.shape
    h = rmsnorm(x, norm_g)
    z = h @ w_in
    splits = np.cumsum([D_POOL, D_POOL, D_CONV, D_CONV, D_CONV, D_XATTN]).tolist()
    u_a, gate_a, glu_val, glu_gate, gate_b, q, gate_c = jnp.split(z, splits, axis=-1)
    o_a, new_pool = pool_mix(u_a, pool_state, start_pos, w_pool, pool_scale)
    o_a = o_a * jax.nn.silu(gate_a)
    a = glu_val * jax.nn.sigmoid(glu_gate)
    o_b, new_conv = conv_module(a, conv_state, w_dw, b_dw, conv_ln_g, conv_ln_b, w_pw)
    o_b = o_b * jax.nn.silu(gate_b)
    o_c = memory_attention(q.reshape(b, t, N_XHEADS, XHEAD_DIM), mem_k, mem_v) * jax.nn.silu(gate_c)
    y = x + jnp.concatenate([o_a, o_b, o_c], axis=-1) @ w_out
    return y, new_pool, new_conv


def setup_inputs(seed: int = 0) -> dict:
    key = jax.random.key(seed)
    ks = jax.random.split(key, 24)
    f32 = jnp.float32
    nrm = lambda k, shape, scale: jax.random.normal(k, shape, f32) * scale
    return {
        "x_prompt": nrm(ks[0], (BATCH, SEQ, D_MODEL), 1.0),
        "mem_prompt": nrm(ks[1], (BATCH, N_MEM, D_MODEL), 1.0),
        "x_sample": nrm(ks[2], (DEC_BATCH, DEC_SEQ, D_MODEL), 1.0),
        "cache_mem_k": nrm(ks[3], (DEPTH, DEC_BATCH, N_MEM, N_XHEADS, XHEAD_DIM), 1.0),
        "cache_mem_v": nrm(ks[4], (DEPTH, DEC_BATCH, N_MEM, N_XHEADS, XHEAD_DIM), 1.0),
        "state_pool": nrm(ks[5], (DEPTH, DEC_BATCH, POOL_STATE, D_POOL), 1.0),
        "state_conv": nrm(ks[6], (DEPTH, DEC_BATCH, CONV_STATE, D_CONV), 0.5),
        "norm_g": 1.0 + nrm(ks[7], (DEPTH, D_MODEL), 0.02),
        "mem_norm_g": 1.0 + nrm(ks[8], (DEPTH, D_MODEL), 0.02),
        "w_in": nrm(ks[9], (DEPTH, D_MODEL, D_IN), D_MODEL ** -0.5),
        "w_mem_k": nrm(ks[10], (DEPTH, D_MODEL, D_XATTN), D_MODEL ** -0.5),
        "w_mem_v": nrm(ks[11], (DEPTH, D_MODEL, D_XATTN), D_MODEL ** -0.5),
        "w_pool": nrm(ks[12], (DEPTH, N_POOL_GROUPS, POOL_GROUP, POOL_GROUP), POOL_GROUP ** -0.5),
        "pool_scale": 1.0 + nrm(ks[13], (DEPTH, D_POOL), 0.02),
        "w_dw": nrm(ks[14], (DEPTH, CONV_WIDTH, D_CONV), CONV_WIDTH ** -0.5),
        "b_dw": nrm(ks[15], (DEPTH, D_CONV), 0.01),
        "conv_ln_g": 1.0 + nrm(ks[16], (DEPTH, D_CONV), 0.02),
        "conv_ln_b": nrm(ks[17], (DEPTH, D_CONV), 0.01),
        "w_pw": nrm(ks[18], (DEPTH, D_CONV, D_CONV), D_CONV ** -0.5),
        "w_out": nrm(ks[19], (DEPTH, D_MODEL, D_MODEL), D_MODEL ** -0.5),
        "final_norm_g": 1.0 + nrm(ks[20], (D_MODEL,), 0.02),
    }


def reference(x_prompt, mem_prompt, x_sample, cache_mem_k, cache_mem_v, state_pool, state_conv,
              norm_g, mem_norm_g, w_in, w_mem_k, w_mem_v, w_pool, pool_scale, w_dw, b_dw,
              conv_ln_g, conv_ln_b, w_pw, w_out, final_norm_g):
    hp, hs = x_prompt, x_sample
    mk_p, mv_p, pool_p, conv_p, pool_s, conv_s = [], [], [], [], [], []
    for l in range(DEPTH):
        k_p, v_p = memory_kv(mem_prompt, mem_norm_g[l], w_mem_k[l], w_mem_v[l])
        zero_pool = jnp.zeros((BATCH, POOL_STATE, D_POOL), hp.dtype)
        zero_conv = jnp.zeros((BATCH, CONV_STATE, D_CONV), hp.dtype)
        hp, np_p, nc_p = hybrid_layer(hp, k_p, v_p, zero_pool, zero_conv, 0, norm_g[l], w_in[l], w_pool[l],
                                      pool_scale[l], w_dw[l], b_dw[l], conv_ln_g[l], conv_ln_b[l], w_pw[l], w_out[l])
        hs, np_s, nc_s = hybrid_layer(hs, cache_mem_k[l], cache_mem_v[l], state_pool[l], state_conv[l], PAST_LEN,
                                      norm_g[l], w_in[l], w_pool[l], pool_scale[l], w_dw[l], b_dw[l],
                                      conv_ln_g[l], conv_ln_b[l], w_pw[l], w_out[l])
        mk_p.append(k_p)
        mv_p.append(v_p)
        pool_p.append(np_p)
        conv_p.append(nc_p)
        pool_s.append(np_s)
        conv_s.append(nc_s)
    y_prompt = rmsnorm(hp, final_norm_g)
    y_sample = rmsnorm(hs, final_norm_g)
    return (y_prompt, y_sample, jnp.stack(mk_p), jnp.stack(mv_p), jnp.stack(pool_p), jnp.stack(conv_p),
            jnp.stack(pool_s), jnp.stack(conv_s))
```

```python
import functools

import jax
import jax.numpy as jnp
import numpy as np
from jax import lax
from jax.experimental import pallas as pl
from jax.experimental.pallas import tpu as pltpu

EPS = 1e-6
POOL_WINDOWS = (2, 4, 8, 16)
CONV_WIDTH = 31
N_XHEADS = 4
PAST_LEN = 16384

SUBLANES = 8
LANES = 128
VMEM_LIMIT_BYTES = 56 * 1024 * 1024

BF16 = jnp.bfloat16
F32 = jnp.float32


def _silu(x):
    return x * jax.nn.sigmoid(x)


def _compiler_params(semantics):
    return pltpu.CompilerParams(dimension_semantics=semantics, vmem_limit_bytes=VMEM_LIMIT_BYTES)


def _norm_matmul_kernel(x_ref, g_ref, w_ref, o_ref, h_ref):
    @pl.when(pl.program_id(1) == 0)
    def _():
        x = x_ref[...]
        ms = jnp.mean(x * x, axis=-1, keepdims=True)
        h_ref[...] = (x * lax.rsqrt(ms + EPS) * g_ref[...]).astype(BF16)

    o_ref[...] = jnp.dot(h_ref[...], w_ref[...], preferred_element_type=F32)


def _norm_matmul(x, g, w, *, tm, tn):
    m, k = x.shape
    n = w.shape[1]
    assert m % tm == 0 and n % tn == 0
    return pl.pallas_call(
        _norm_matmul_kernel,
        grid=(m // tm, n // tn),
        in_specs=[
            pl.BlockSpec((tm, k), lambda i, j: (i, 0)),
            pl.BlockSpec((1, k), lambda i, j: (0, 0)),
            pl.BlockSpec((k, tn), lambda i, j: (0, j)),
        ],
        out_specs=pl.BlockSpec((tm, tn), lambda i, j: (i, j)),
        out_shape=jax.ShapeDtypeStruct((m, n), F32),
        scratch_shapes=[pltpu.VMEM((tm, k), BF16)],
        compiler_params=_compiler_params(("arbitrary", "arbitrary")),
        name="norm_matmul",
    )(x, g, w)


def _shift_up(lo, hi, s, row):
    if s == 0:
        return lo
    return jnp.where(row < SUBLANES - s, pltpu.roll(lo, SUBLANES - s, axis=0), pltpu.roll(hi, SUBLANES - s, axis=0))


def _shift_down(prev, cur, d, row):
    if d == 0:
        return cur
    return jnp.where(row >= d, pltpu.roll(cur, d, axis=0), pltpu.roll(prev, d, axis=0))


POOL_HALO = 16
CONV_HALO = 32
CONV_ROWS = 32


def _pool_window_sums(extp_ref, pooled_ref, tt, d_pool, t_idx):
    pg = d_pool // len(POOL_WINDOWS)
    row = lax.broadcasted_iota(jnp.int32, (SUBLANES, LANES), 0)
    n_slab = tt // SUBLANES
    for gi, w in enumerate(POOL_WINDOWS):
        n_double = int(np.log2(w))
        for cb in range(pg // LANES):
            c0 = gi * pg + cb * LANES

            def body(i, carry, c0=c0, w=w, n_double=n_double):
                n_prev = 2 if w > SUBLANES else 1
                base = pl.multiple_of(POOL_HALO + i * SUBLANES, SUBLANES)
                a = [extp_ref[pl.ds(base - (n_prev - j) * SUBLANES, SUBLANES), c0:c0 + LANES]
                     for j in range(n_prev + 1)]
                cur = a[-1]
                d = 1
                for _ in range(n_double):
                    if d < SUBLANES:
                        a = [a[j] + _shift_down(a[max(j - 1, 0)], a[j], d, row) for j in range(len(a))]
                    else:
                        a = [a[j] + a[j - 1] for j in range(1, len(a))]
                    d *= 2
                s = a[-1]
                pos = t_idx * tt + i * SUBLANES + row
                cnt = jnp.minimum(w, pos + 1).astype(F32)
                pooled_ref[pl.ds(pl.multiple_of(i * SUBLANES, SUBLANES), SUBLANES), c0:c0 + LANES] = s / cnt - cur
                return carry

            lax.fori_loop(0, n_slab, body, 0)


def _dwconv_tile(extc_ref, wdw_ref, bdw_ref, y_ref, tt, d_conv):
    row = lax.broadcasted_iota(jnp.int32, (SUBLANES, LANES), 0)
    n_out = CONV_ROWS // SUBLANES
    q_max = (CONV_WIDTH + 1) // SUBLANES + 1
    for cb in range(d_conv // LANES):
        c0 = cb * LANES

        def body(i, carry, c0=c0):
            base = pl.multiple_of(i * CONV_ROWS, CONV_ROWS)
            slabs = [extc_ref[pl.ds(base + j * SUBLANES, SUBLANES), c0:c0 + LANES] for j in range(n_out + q_max - 1)]
            acc = [jnp.broadcast_to(bdw_ref[:, c0:c0 + LANES], (SUBLANES, LANES)) for _ in range(n_out)]
            for s in range(SUBLANES):
                taps = [(q, SUBLANES * q + s - 2) for q in range(q_max) if 0 <= SUBLANES * q + s - 2 < CONV_WIDTH]
                part = []
                for j in range(n_out + (1 if s else 0)):
                    p = None
                    for q, k in taps:
                        term = slabs[j + q] * wdw_ref[k:k + 1, c0:c0 + LANES]
                        p = term if p is None else p + term
                    part.append(p)
                if s == 0:
                    for j in range(n_out):
                        acc[j] = acc[j] + part[j]
                else:
                    rolled = [pltpu.roll(p, SUBLANES - s, axis=0) for p in part]
                    for j in range(n_out):
                        acc[j] = acc[j] + jnp.where(row < SUBLANES - s, rolled[j], rolled[j + 1])
            for j in range(n_out):
                y_ref[pl.ds(base + j * SUBLANES, SUBLANES), c0:c0 + LANES] = acc[j]
            return carry

        lax.fori_loop(0, tt // CONV_ROWS, body, 0)


def _layernorm(y, g, b):
    mu = jnp.mean(y, axis=-1, keepdims=True)
    d = y - mu
    var = jnp.mean(d * d, axis=-1, keepdims=True)
    return d * lax.rsqrt(var + EPS) * g + b


def _attend(q, k, v, xhead_dim):
    s = lax.dot_general(q.astype(BF16), k, (((1,), (1,)), ((), ())), preferred_element_type=F32)
    s = s * (xhead_dim ** -0.5)
    m = jnp.max(s, axis=-1, keepdims=True)
    e = jnp.exp(s - m)
    l = jnp.sum(e, axis=-1, keepdims=True)
    o = jnp.dot(e.astype(BF16), v, preferred_element_type=F32)
    return o / l


def _prompt_mixer_kernel(z_ref, k_ref, v_ref, wpool_ref, pscale_ref, wdw_ref, bdw_ref, lng_ref, lnb_ref, wpw_ref,
                         o_ref, pstate_ref, cstate_ref, extp_ref, extc_ref, pooled_ref, y_ref,
                         *, tt, d_pool, d_conv, d_xattn):
    t_idx = pl.program_id(1)
    n_t = pl.num_programs(1)
    pg = d_pool // len(POOL_WINDOWS)
    xhead_dim = d_xattn // N_XHEADS
    off_gate_a = d_pool
    off_val = 2 * d_pool
    off_glu = off_val + d_conv
    off_gate_b = off_glu + d_conv
    off_q = off_gate_b + d_conv
    off_gate_c = off_q + d_xattn

    @pl.when(t_idx == 0)
    def _():
        extp_ref[0:POOL_HALO, :] = jnp.zeros((POOL_HALO, d_pool), F32)
        extc_ref[0:CONV_HALO, :] = jnp.zeros((CONV_HALO, d_conv), F32)

    extp_ref[POOL_HALO:POOL_HALO + tt, :] = z_ref[:, 0:d_pool]
    _pool_window_sums(extp_ref, pooled_ref, tt, d_pool, t_idx)
    for gi in range(len(POOL_WINDOWS)):
        c0, c1 = gi * pg, (gi + 1) * pg
        mixed = jnp.dot(pooled_ref[:, c0:c1].astype(BF16), wpool_ref[gi], preferred_element_type=F32)
        mixed = mixed * pscale_ref[:, c0:c1]
        o_ref[:, c0:c1] = (mixed * _silu(z_ref[:, off_gate_a + c0:off_gate_a + c1])).astype(BF16)

    extc_ref[CONV_HALO:CONV_HALO + tt, :] = z_ref[:, off_val:off_val + d_conv] * jax.nn.sigmoid(
        z_ref[:, off_glu:off_glu + d_conv])
    _dwconv_tile(extc_ref, wdw_ref, bdw_ref, y_ref, tt, d_conv)
    act = _silu(_layernorm(y_ref[...], lng_ref[...], lnb_ref[...]))
    o_b = jnp.dot(act.astype(BF16), wpw_ref[...], preferred_element_type=F32)
    o_ref[:, d_pool:d_pool + d_conv] = (o_b * _silu(z_ref[:, off_gate_b:off_gate_b + d_conv])).astype(BF16)

    for h in range(N_XHEADS):
        c0, c1 = h * xhead_dim, (h + 1) * xhead_dim
        o_c = _attend(z_ref[:, off_q + c0:off_q + c1], k_ref[0, :, c0:c1], v_ref[0, :, c0:c1], xhead_dim)
        o_c = o_c * _silu(z_ref[:, off_gate_c + c0:off_gate_c + c1])
        o_ref[:, d_pool + d_conv + c0:d_pool + d_conv + c1] = o_c.astype(BF16)

    @pl.when(t_idx == n_t - 1)
    def _():
        pstate_ref[0] = extp_ref[tt + 1:tt + POOL_HALO, :]
        cstate_ref[0] = extc_ref[tt + 2:tt + CONV_HALO, :]

    extp_ref[0:POOL_HALO, :] = extp_ref[tt:tt + POOL_HALO, :]
    extc_ref[0:CONV_HALO, :] = extc_ref[tt:tt + CONV_HALO, :]


def _prompt_mixer(z, k, v, w_pool, pool_scale, w_dw, b_dw, ln_g, ln_b, w_pw, *, batch, seq, tt, d_model):
    d_in = z.shape[1]
    d_pool = w_pool.shape[0] * w_pool.shape[1]
    d_conv = w_pw.shape[0]
    d_xattn = k.shape[2]
    n_mem = k.shape[1]
    n_t = seq // tt
    assert seq % tt == 0 and tt % CONV_ROWS == 0
    kernel = functools.partial(_prompt_mixer_kernel, tt=tt, d_pool=d_pool, d_conv=d_conv, d_xattn=d_xattn)
    const2 = lambda b, t: (0, 0)
    return pl.pallas_call(
        kernel,
        grid=(batch, n_t),
        in_specs=[
            pl.BlockSpec((tt, d_in), lambda b, t: (b * n_t + t, 0)),
            pl.BlockSpec((1, n_mem, d_xattn), lambda b, t: (b, 0, 0)),
            pl.BlockSpec((1, n_mem, d_xattn), lambda b, t: (b, 0, 0)),
            pl.BlockSpec(w_pool.shape, lambda b, t: (0, 0, 0)),
            pl.BlockSpec((1, d_pool), const2),
            pl.BlockSpec(w_dw.shape, const2),
            pl.BlockSpec((1, d_conv), const2),
            pl.BlockSpec((1, d_conv), const2),
            pl.BlockSpec((1, d_conv), const2),
            pl.BlockSpec(w_pw.shape, const2),
        ],
        out_specs=[
            pl.BlockSpec((tt, d_model), lambda b, t: (b * n_t + t, 0)),
            pl.BlockSpec((1, POOL_HALO - 1, d_pool), lambda b, t: (b, 0, 0)),
            pl.BlockSpec((1, CONV_HALO - 2, d_conv), lambda b, t: (b, 0, 0)),
        ],
        out_shape=[
            jax.ShapeDtypeStruct((batch * seq, d_model), BF16),
            jax.ShapeDtypeStruct((batch, POOL_HALO - 1, d_pool), F32),
            jax.ShapeDtypeStruct((batch, CONV_HALO - 2, d_conv), F32),
        ],
        scratch_shapes=[
            pltpu.VMEM((POOL_HALO + tt, d_pool), F32),
            pltpu.VMEM((CONV_HALO + tt, d_conv), F32),
            pltpu.VMEM((tt, d_pool), F32),
            pltpu.VMEM((tt, d_conv), F32),
        ],
        compiler_params=_compiler_params(("arbitrary", "arbitrary")),
        name="prompt_mixer",
    )(z, k, v, w_pool, pool_scale, w_dw, b_dw, ln_g, ln_b, w_pw)


def _sample_mixer_kernel(z_ref, pst_ref, cst_ref, wpool_ref, pscale_ref, wdw_ref, bdw_ref, lng_ref, lnb_ref, wpw_ref,
                         o_ref, pnew_ref, cnew_ref, *, dec_seq, d_pool, d_conv):
    pg = d_pool // len(POOL_WINDOWS)
    n_pst = pst_ref.shape[1]
    n_cst = cst_ref.shape[1]
    gb = z_ref.shape[1]
    off_gate_a = d_pool
    off_val = 2 * d_pool
    off_glu = off_val + d_conv
    off_gate_b = off_glu + d_conv

    def pool_row(r, c0, c1):
        if r < n_pst:
            return pst_ref[:, r, c0:c1]
        return z_ref[r - n_pst, :, c0:c1]

    for gi, w in enumerate(POOL_WINDOWS):
        c0, c1 = gi * pg, (gi + 1) * pg
        pooled = []
        for t in range(dec_seq):
            s = pool_row(n_pst + t, c0, c1)
            cur = s
            for k in range(1, w):
                s = s + pool_row(n_pst + t - k, c0, c1)
            cnt = float(min(w, PAST_LEN + t + 1))
            pooled.append((s / cnt - cur).astype(BF16))
        mixed = jnp.dot(jnp.concatenate(pooled, axis=0), wpool_ref[gi], preferred_element_type=F32)
        mixed = mixed * pscale_ref[:, c0:c1]
        for t in range(dec_seq):
            gate = z_ref[t, :, off_gate_a + c0:off_gate_a + c1]
            o_ref[t, :, c0:c1] = (mixed[t * gb:(t + 1) * gb] * _silu(gate)).astype(BF16)
    for r in range(n_pst):
        pnew_ref[:, r, :] = pool_row(r + dec_seq, 0, d_pool)

    cw = 2 * LANES
    ys = [[] for _ in range(dec_seq)]
    for cb in range(d_conv // cw):
        c0, c1 = cb * cw, (cb + 1) * cw
        a_new = [z_ref[t, :, off_val + c0:off_val + c1] * jax.nn.sigmoid(z_ref[t, :, off_glu + c0:off_glu + c1])
                 for t in range(dec_seq)]

        def conv_row(r, c0=c0, c1=c1, a_new=a_new):
            if r < n_cst:
                return cst_ref[:, r, c0:c1]
            return a_new[r - n_cst]

        acc = [jnp.broadcast_to(bdw_ref[:, c0:c1], (gb, cw)) for _ in range(dec_seq)]
        for r in range(n_cst + dec_seq):
            x_r = conv_row(r)
            for t in range(dec_seq):
                k = r - t
                if 0 <= k < CONV_WIDTH:
                    acc[t] = acc[t] + x_r * wdw_ref[k:k + 1, c0:c1]
            if r >= dec_seq:
                cnew_ref[:, r - dec_seq, c0:c1] = x_r
        for t in range(dec_seq):
            ys[t].append(acc[t])
    y = jnp.concatenate([jnp.concatenate(ys[t], axis=1) for t in range(dec_seq)], axis=0)
    act = _silu(_layernorm(y, lng_ref[...], lnb_ref[...]))
    o_b = jnp.dot(act.astype(BF16), wpw_ref[...], preferred_element_type=F32)
    for t in range(dec_seq):
        gate = z_ref[t, :, off_gate_b:off_gate_b + d_conv]
        o_ref[t, :, d_pool:d_pool + d_conv] = (o_b[t * gb:(t + 1) * gb] * _silu(gate)).astype(BF16)


def _sample_mixer(z3, pstate, cstate, w_pool, pool_scale, w_dw, b_dw, ln_g, ln_b, w_pw, *, gb):
    dec_seq, dec_batch, d_in = z3.shape
    d_pool = pstate.shape[2]
    d_conv = cstate.shape[2]
    assert dec_batch % gb == 0
    kernel = functools.partial(_sample_mixer_kernel, dec_seq=dec_seq, d_pool=d_pool, d_conv=d_conv)
    const2 = lambda i: (0, 0)
    return pl.pallas_call(
        kernel,
        grid=(dec_batch // gb,),
        in_specs=[
            pl.BlockSpec((dec_seq, gb, d_in), lambda i: (0, i, 0)),
            pl.BlockSpec((gb,) + pstate.shape[1:], lambda i: (i, 0, 0)),
            pl.BlockSpec((gb,) + cstate.shape[1:], lambda i: (i, 0, 0)),
            pl.BlockSpec(w_pool.shape, lambda i: (0, 0, 0)),
            pl.BlockSpec((1, d_pool), const2),
            pl.BlockSpec(w_dw.shape, const2),
            pl.BlockSpec((1, d_conv), const2),
            pl.BlockSpec((1, d_conv), const2),
            pl.BlockSpec((1, d_conv), const2),
            pl.BlockSpec(w_pw.shape, const2),
        ],
        out_specs=[
            pl.BlockSpec((dec_seq, gb, d_pool + d_conv), lambda i: (0, i, 0)),
            pl.BlockSpec((gb,) + pstate.shape[1:], lambda i: (i, 0, 0)),
            pl.BlockSpec((gb,) + cstate.shape[1:], lambda i: (i, 0, 0)),
        ],
        out_shape=[
            jax.ShapeDtypeStruct((dec_seq, dec_batch, d_pool + d_conv), BF16),
            jax.ShapeDtypeStruct(pstate.shape, F32),
            jax.ShapeDtypeStruct(cstate.shape, F32),
        ],
        compiler_params=_compiler_params(("arbitrary",)),
        name="sample_mixer",
    )(z3, pstate, cstate, w_pool, pool_scale, w_dw, b_dw, ln_g, ln_b, w_pw)


def _sample_attn_kernel(q_ref, gate_ref, k_ref, v_ref, o_ref, *, xhead_dim):
    gb = k_ref.shape[0]
    for b in range(gb):
        for h in range(N_XHEADS):
            c0, c1 = h * xhead_dim, (h + 1) * xhead_dim
            o_c = _attend(q_ref[:, b, c0:c1], k_ref[b, :, c0:c1].astype(BF16), v_ref[b, :, c0:c1].astype(BF16),
                          xhead_dim)
            o_ref[:, b, c0:c1] = o_c * _silu(gate_ref[:, b, c0:c1])


def _sample_attn(q3, gate3, cache_k, cache_v, *, gb):
    dec_seq, dec_batch, d_xattn = q3.shape
    n_mem = cache_k.shape[1]
    kernel = functools.partial(_sample_attn_kernel, xhead_dim=d_xattn // N_XHEADS)
    return pl.pallas_call(
        kernel,
        grid=(dec_batch // gb,),
        in_specs=[
            pl.BlockSpec((dec_seq, gb, d_xattn), lambda i: (0, i, 0)),
            pl.BlockSpec((dec_seq, gb, d_xattn), lambda i: (0, i, 0)),
            pl.BlockSpec((gb, n_mem, d_xattn), lambda i: (i, 0, 0)),
            pl.BlockSpec((gb, n_mem, d_xattn), lambda i: (i, 0, 0)),
        ],
        out_specs=pl.BlockSpec((dec_seq, gb, d_xattn), lambda i: (0, i, 0)),
        out_shape=jax.ShapeDtypeStruct((dec_seq, dec_batch, d_xattn), F32),
        compiler_params=_compiler_params(("arbitrary",)),
        name="sample_attn",
    )(q3, gate3, cache_k, cache_v)


def _out_proj_kernel(o_ref, w_ref, x_ref, g_ref, y_ref):
    kk = pl.program_id(1)
    part = jnp.dot(o_ref[...], w_ref[...], preferred_element_type=F32)

    @pl.when(kk == 0)
    def _():
        y_ref[...] = x_ref[...] + part

    @pl.when(kk > 0)
    def _():
        y_ref[...] += part

    @pl.when(kk == pl.num_programs(1) - 1)
    def _():
        y = y_ref[...]
        ms = jnp.mean(y * y, axis=-1, keepdims=True)
        y_ref[...] = y * lax.rsqrt(ms + EPS) * g_ref[...]


def _out_proj(o, w, x, g, *, tm, tk):
    m, k = o.shape
    n = w.shape[1]
    assert m % tm == 0 and k % tk == 0
    return pl.pallas_call(
        _out_proj_kernel,
        grid=(m // tm, k // tk),
        in_specs=[
            pl.BlockSpec((tm, tk), lambda i, kk: (i, kk)),
            pl.BlockSpec((tk, n), lambda i, kk: (kk, 0)),
            pl.BlockSpec((tm, n), lambda i, kk: (i, 0)),
            pl.BlockSpec((1, n), lambda i, kk: (0, 0)),
        ],
        out_specs=pl.BlockSpec((tm, n), lambda i, kk: (i, 0)),
        out_shape=jax.ShapeDtypeStruct((m, n), F32),
        compiler_params=_compiler_params(("arbitrary", "arbitrary")),
        name="out_proj",
    )(o, w, x, g)


def kernel(x_prompt, mem_prompt, x_sample, cache_mem_k, cache_mem_v, state_pool, state_conv, norm_g, mem_norm_g, w_in,
           w_mem_k, w_mem_v, w_pool, pool_scale, w_dw, b_dw, conv_ln_g, conv_ln_b, w_pw, w_out, final_norm_g):
    depth = w_in.shape[0]
    assert depth == 1, "single-layer step"
    batch, seq, d_model = x_prompt.shape
    dec_batch, dec_seq, _ = x_sample.shape
    n_mem = mem_prompt.shape[1]
    d_xattn = w_mem_k.shape[2]
    d_pool = pool_scale.shape[1]
    d_conv = w_pw.shape[1]
    xhead_dim = d_xattn // N_XHEADS
    off_q = 2 * d_pool + 3 * d_conv
    l = 0

    w_in_b = w_in[l].astype(BF16)
    w_out_b = w_out[l].astype(BF16)
    w_mem_b = jnp.concatenate([w_mem_k[l], w_mem_v[l]], axis=1).astype(BF16)
    w_pool_b = w_pool[l].astype(BF16)
    w_pw_b = w_pw[l].astype(BF16)
    g_in = norm_g[l][None, :]
    g_mem = mem_norm_g[l][None, :]
    g_fin = final_norm_g[None, :]
    pscale = pool_scale[l][None, :]
    bdw = b_dw[l][None, :]
    lng = conv_ln_g[l][None, :]
    lnb = conv_ln_b[l][None, :]
    mixer_w = (w_pool_b, pscale, w_dw[l], bdw, lng, lnb, w_pw_b)

    xp = x_prompt.reshape(batch * seq, d_model)
    kv = _norm_matmul(mem_prompt.reshape(batch * n_mem, d_model), g_mem, w_mem_b, tm=512, tn=512)
    k_p = kv[:, :d_xattn].reshape(batch, n_mem, d_xattn)
    v_p = kv[:, d_xattn:].reshape(batch, n_mem, d_xattn)
    z_p = _norm_matmul(xp, g_in, w_in_b, tm=512, tn=512)
    o_p, pool_p, conv_p = _prompt_mixer(z_p, k_p.astype(BF16), v_p.astype(BF16), *mixer_w,
                                        batch=batch, seq=seq, tt=256, d_model=d_model)
    y_p = _out_proj(o_p, w_out_b, xp, g_fin, tm=256, tk=512).reshape(batch, seq, d_model)

    xs = x_sample.transpose(1, 0, 2).reshape(dec_seq * dec_batch, d_model)
    z_s = _norm_matmul(xs, g_in, w_in_b, tm=512, tn=512).reshape(dec_seq, dec_batch, -1)
    o_ab, pool_s, conv_s = _sample_mixer(z_s, state_pool[l], state_conv[l], *mixer_w, gb=16)
    o_c = _sample_attn(z_s[:, :, off_q:off_q + d_xattn], z_s[:, :, off_q + d_xattn:],
                       cache_mem_k[l].reshape(dec_batch, n_mem, d_xattn),
                       cache_mem_v[l].reshape(dec_batch, n_mem, d_xattn), gb=SUBLANES)
    o_s = jnp.concatenate([o_ab, o_c.astype(BF16)], axis=-1).reshape(dec_seq * dec_batch, d_model)
    y_s = _out_proj(o_s, w_out_b, xs, g_fin, tm=256, tk=512)
    y_s = y_s.reshape(dec_seq, dec_batch, d_model).transpose(1, 0, 2)

    mem_shape = (depth, batch, n_mem, N_XHEADS, xhead_dim)
    return (y_p, y_s, k_p.reshape(mem_shape), v_p.reshape(mem_shape), pool_p[None], conv_p[None], pool_s[None],
            conv_s[None])
```

```python
import functools

import jax
import jax.numpy as jnp
from jax import lax
from jax.experimental import pallas as pl
from jax.experimental.pallas import tpu as pltpu

EPS = 1e-6
POOL_WINDOWS = (2, 4, 8, 16)
CONV_WIDTH = 31
N_XHEADS = 4
PAST_LEN = 16384

SUBLANES = 8
LANES = 128
VMEM_LIMIT_BYTES = 56 * 1024 * 1024

PROMPT_PROJ_TILE = (1024, 512)
FUSED_PROJ_TILE = (512, 512)
NORM_ROWS = 512
OUT_PROJ_TILE = (1024, 512)
SAMPLE_OUT_PROJ_TILE = (512, 512)
PROMPT_MIXER_ROWS = 256
POOL_CHUNK = 32
CONV_CHUNK = 64
SAMPLE_MIXER_SEQS = 16
SAMPLE_ATTN_SEQS = 4

BF16 = jnp.bfloat16
F32 = jnp.float32


def _silu(x):
    return x * jax.nn.sigmoid(x)


def _rms_scale(x):
    return lax.rsqrt(jnp.mean(x * x, axis=-1, keepdims=True) + EPS)


def _compiler_params(semantics):
    return pltpu.CompilerParams(dimension_semantics=semantics, vmem_limit_bytes=VMEM_LIMIT_BYTES)


def _rmsnorm_kernel(x_ref, g_ref, h_ref):
    x = x_ref[...]
    h_ref[...] = (x * _rms_scale(x) * g_ref[...]).astype(BF16)


def _rmsnorm_bf16(x, g, *, tm):
    m, k = x.shape
    return pl.pallas_call(
        _rmsnorm_kernel,
        grid=(m // tm,),
        in_specs=[pl.BlockSpec((tm, k), lambda i: (i, 0)), pl.BlockSpec((1, k), lambda i: (0, 0))],
        out_specs=pl.BlockSpec((tm, k), lambda i: (i, 0)),
        out_shape=jax.ShapeDtypeStruct((m, k), BF16),
        compiler_params=_compiler_params(("arbitrary",)),
        name="rmsnorm",
    )(x, g)


def _matmul_kernel(h_ref, w_ref, o_ref):
    o_ref[...] = jnp.dot(h_ref[...], w_ref[...], preferred_element_type=F32)


def _matmul(h, w, *, tile):
    tm, tn = tile
    m, k = h.shape
    n = w.shape[1]
    assert m % tm == 0 and n % tn == 0
    return pl.pallas_call(
        _matmul_kernel,
        grid=(m // tm, n // tn),
        in_specs=[pl.BlockSpec((tm, k), lambda i, j: (i, 0)), pl.BlockSpec((k, tn), lambda i, j: (0, j))],
        out_specs=pl.BlockSpec((tm, tn), lambda i, j: (i, j)),
        out_shape=jax.ShapeDtypeStruct((m, n), F32),
        compiler_params=_compiler_params(("arbitrary", "arbitrary")),
        name="proj_matmul",
    )(h, w)


def _norm_matmul_kernel(x_ref, g_ref, w_ref, o_ref, h_ref):
    @pl.when(pl.program_id(1) == 0)
    def _():
        x = x_ref[...]
        h_ref[...] = (x * _rms_scale(x) * g_ref[...]).astype(BF16)

    o_ref[...] = jnp.dot(h_ref[...], w_ref[...], preferred_element_type=F32)


def _norm_matmul(x, g, w, *, tile):
    tm, tn = tile
    m, k = x.shape
    n = w.shape[1]
    assert m % tm == 0 and n % tn == 0
    return pl.pallas_call(
        _norm_matmul_kernel,
        grid=(m // tm, n // tn),
        in_specs=[
            pl.BlockSpec((tm, k), lambda i, j: (i, 0)),
            pl.BlockSpec((1, k), lambda i, j: (0, 0)),
            pl.BlockSpec((k, tn), lambda i, j: (0, j)),
        ],
        out_specs=pl.BlockSpec((tm, tn), lambda i, j: (i, j)),
        out_shape=jax.ShapeDtypeStruct((m, n), F32),
        scratch_shapes=[pltpu.VMEM((tm, k), BF16)],
        compiler_params=_compiler_params(("arbitrary", "arbitrary")),
        name="norm_matmul",
    )(x, g, w)


POOL_HALO = 16
CONV_HALO = 32


def _pool_tile(extp_ref, pooled_ref, tt, d_pool, t_idx):
    pg = d_pool // len(POOL_WINDOWS)
    row = lax.broadcasted_iota(jnp.int32, (POOL_CHUNK, LANES), 0)
    for gi, w in enumerate(POOL_WINDOWS):
        for cb in range(gi * pg // LANES, (gi + 1) * pg // LANES):

            def body(i, carry, cb=cb, w=w):
                base = pl.multiple_of(i * POOL_CHUNK, POOL_CHUNK)
                cur = extp_ref[cb, pl.ds(base + POOL_HALO, POOL_CHUNK), :]
                s = cur
                for k in range(1, w):
                    s = s + extp_ref[cb, pl.ds(base + POOL_HALO - k, POOL_CHUNK), :]
                cnt = jnp.minimum(w, t_idx * tt + base + row + 1).astype(F32)
                pooled_ref[pl.ds(base, POOL_CHUNK), cb * LANES:(cb + 1) * LANES] = s / cnt - cur
                return carry

            lax.fori_loop(0, tt // POOL_CHUNK, body, 0)


def _dwconv_tile(extc_ref, wdw_ref, bdw_ref, y_ref, tt, d_conv):
    first = CONV_HALO - (CONV_WIDTH - 1)
    for cb in range(d_conv // LANES):
        c0, c1 = cb * LANES, (cb + 1) * LANES

        def body(i, carry, cb=cb, c0=c0, c1=c1):
            base = pl.multiple_of(i * CONV_CHUNK, CONV_CHUNK)
            acc = jnp.broadcast_to(bdw_ref[:, c0:c1], (CONV_CHUNK, LANES))
            for k in range(CONV_WIDTH):
                acc = acc + extc_ref[cb, pl.ds(base + first + k, CONV_CHUNK), :] * wdw_ref[k:k + 1, c0:c1]
            y_ref[pl.ds(base, CONV_CHUNK), c0:c1] = acc
            return carry

        lax.fori_loop(0, tt // CONV_CHUNK, body, 0)


def _layernorm(y, g, b):
    mu = jnp.mean(y, axis=-1, keepdims=True)
    d = y - mu
    var = jnp.mean(d * d, axis=-1, keepdims=True)
    return d * lax.rsqrt(var + EPS) * g + b


def _attend(q, k, v, xhead_dim):
    s = lax.dot_general(q.astype(BF16), k, (((1,), (1,)), ((), ())), preferred_element_type=F32)
    s = s * (xhead_dim ** -0.5)
    e = jnp.exp(s - jnp.max(s, axis=-1, keepdims=True))
    l = jnp.sum(e, axis=-1, keepdims=True)
    return jnp.dot(e.astype(BF16), v, preferred_element_type=F32) / l


def _prompt_mixer_kernel(z_ref, k_ref, v_ref, wpool_ref, pscale_ref, wdw_ref, bdw_ref, lng_ref, lnb_ref, wpw_ref,
                         o_ref, pstate_ref, cstate_ref, extp_ref, extc_ref, pooled_ref, y_ref,
                         *, tt, d_pool, d_conv, d_xattn):
    t_idx = pl.program_id(1)
    n_t = pl.num_programs(1)
    pg = d_pool // len(POOL_WINDOWS)
    xhead_dim = d_xattn // N_XHEADS
    off_gate_a = d_pool
    off_val = 2 * d_pool
    off_glu = off_val + d_conv
    off_gate_b = off_glu + d_conv
    off_q = off_gate_b + d_conv
    off_gate_c = off_q + d_xattn

    @pl.when(t_idx == 0)
    def _():
        extp_ref[:, 0:POOL_HALO, :] = jnp.zeros((d_pool // LANES, POOL_HALO, LANES), F32)
        extc_ref[:, 0:CONV_HALO, :] = jnp.zeros((d_conv // LANES, CONV_HALO, LANES), F32)

    for cb in range(d_pool // LANES):
        extp_ref[cb, POOL_HALO:POOL_HALO + tt, :] = z_ref[:, cb * LANES:(cb + 1) * LANES]
    _pool_tile(extp_ref, pooled_ref, tt, d_pool, t_idx)
    for gi in range(len(POOL_WINDOWS)):
        c0, c1 = gi * pg, (gi + 1) * pg
        mixed = jnp.dot(pooled_ref[:, c0:c1].astype(BF16), wpool_ref[gi], preferred_element_type=F32)
        mixed = mixed * pscale_ref[:, c0:c1]
        o_ref[:, c0:c1] = (mixed * _silu(z_ref[:, off_gate_a + c0:off_gate_a + c1])).astype(BF16)

    for cb in range(d_conv // LANES):
        c0, c1 = cb * LANES, (cb + 1) * LANES
        extc_ref[cb, CONV_HALO:CONV_HALO + tt, :] = z_ref[:, off_val + c0:off_val + c1] * jax.nn.sigmoid(
            z_ref[:, off_glu + c0:off_glu + c1])
    _dwconv_tile(extc_ref, wdw_ref, bdw_ref, y_ref, tt, d_conv)
    act = _silu(_layernorm(y_ref[...], lng_ref[...], lnb_ref[...]))
    o_b = jnp.dot(act.astype(BF16), wpw_ref[...], preferred_element_type=F32)
    o_ref[:, d_pool:d_pool + d_conv] = (o_b * _silu(z_ref[:, off_gate_b:off_gate_b + d_conv])).astype(BF16)

    for h in range(N_XHEADS):
        c0, c1 = h * xhead_dim, (h + 1) * xhead_dim
        o_c = _attend(z_ref[:, off_q + c0:off_q + c1], k_ref[0, :, c0:c1], v_ref[0, :, c0:c1], xhead_dim)
        o_c = o_c * _silu(z_ref[:, off_gate_c + c0:off_gate_c + c1])
        o_ref[:, d_pool + d_conv + c0:d_pool + d_conv + c1] = o_c.astype(BF16)

    @pl.when(t_idx == n_t - 1)
    def _():
        for cb in range(d_pool // LANES):
            pstate_ref[0, :, cb * LANES:(cb + 1) * LANES] = extp_ref[cb, tt + 1:tt + POOL_HALO, :]
        for cb in range(d_conv // LANES):
            cstate_ref[0, :, cb * LANES:(cb + 1) * LANES] = extc_ref[cb, tt + 2:tt + CONV_HALO, :]

    extp_ref[:, 0:POOL_HALO, :] = extp_ref[:, tt:tt + POOL_HALO, :]
    extc_ref[:, 0:CONV_HALO, :] = extc_ref[:, tt:tt + CONV_HALO, :]


def _prompt_mixer(z, k, v, w_pool, pool_scale, w_dw, b_dw, ln_g, ln_b, w_pw, *, batch, seq, tt, d_model):
    d_in = z.shape[1]
    d_pool = w_pool.shape[0] * w_pool.shape[1]
    d_conv = w_pw.shape[0]
    d_xattn = k.shape[2]
    n_mem = k.shape[1]
    n_t = seq // tt
    assert seq % tt == 0 and tt % CONV_CHUNK == 0 and tt % POOL_CHUNK == 0
    kernel = functools.partial(_prompt_mixer_kernel, tt=tt, d_pool=d_pool, d_conv=d_conv, d_xattn=d_xattn)
    const2 = lambda b, t: (0, 0)
    return pl.pallas_call(
        kernel,
        grid=(batch, n_t),
        in_specs=[
            pl.BlockSpec((tt, d_in), lambda b, t: (b * n_t + t, 0)),
            pl.BlockSpec((1, n_mem, d_xattn), lambda b, t: (b, 0, 0)),
            pl.BlockSpec((1, n_mem, d_xattn), lambda b, t: (b, 0, 0)),
            pl.BlockSpec(w_pool.shape, lambda b, t: (0, 0, 0)),
            pl.BlockSpec((1, d_pool), const2),
            pl.BlockSpec(w_dw.shape, const2),
            pl.BlockSpec((1, d_conv), const2),
            pl.BlockSpec((1, d_conv), const2),
            pl.BlockSpec((1, d_conv), const2),
            pl.BlockSpec(w_pw.shape, const2),
        ],
        out_specs=[
            pl.BlockSpec((tt, d_model), lambda b, t: (b * n_t + t, 0)),
            pl.BlockSpec((1, POOL_HALO - 1, d_pool), lambda b, t: (b, 0, 0)),
            pl.BlockSpec((1, CONV_HALO - 2, d_conv), lambda b, t: (b, 0, 0)),
        ],
        out_shape=[
            jax.ShapeDtypeStruct((batch * seq, d_model), BF16),
            jax.ShapeDtypeStruct((batch, POOL_HALO - 1, d_pool), F32),
            jax.ShapeDtypeStruct((batch, CONV_HALO - 2, d_conv), F32),
        ],
        scratch_shapes=[
            pltpu.VMEM((d_pool // LANES, POOL_HALO + tt, LANES), F32),
            pltpu.VMEM((d_conv // LANES, CONV_HALO + tt, LANES), F32),
            pltpu.VMEM((tt, d_pool), F32),
            pltpu.VMEM((tt, d_conv), F32),
        ],
        compiler_params=_compiler_params(("arbitrary", "arbitrary")),
        name="prompt_mixer",
    )(z, k, v, w_pool, pool_scale, w_dw, b_dw, ln_g, ln_b, w_pw)


def _sample_mixer_kernel(z_ref, pst_ref, cst_ref, wpool_ref, pscale_ref, wdw_ref, bdw_ref, lng_ref, lnb_ref, wpw_ref,
                         o_ref, pnew_ref, cnew_ref, *, dec_seq, d_pool, d_conv):
    pg = d_pool // len(POOL_WINDOWS)
    n_pst = pst_ref.shape[0]
    n_cst = cst_ref.shape[0]
    gb = z_ref.shape[1]
    off_gate_a = d_pool
    off_val = 2 * d_pool
    off_glu = off_val + d_conv
    off_gate_b = off_glu + d_conv

    def pool_row(r, c0, c1):
        if r < n_pst:
            return pst_ref[r, :, c0:c1]
        return z_ref[r - n_pst, :, c0:c1]

    for gi, w in enumerate(POOL_WINDOWS):
        c0, c1 = gi * pg, (gi + 1) * pg
        pooled = []
        for t in range(dec_seq):
            s = pool_row(n_pst + t, c0, c1)
            cur = s
            for k in range(1, w):
                s = s + pool_row(n_pst + t - k, c0, c1)
            cnt = float(min(w, PAST_LEN + t + 1))
            pooled.append((s / cnt - cur).astype(BF16))
        mixed = jnp.dot(jnp.concatenate(pooled, axis=0), wpool_ref[gi], preferred_element_type=F32)
        mixed = mixed * pscale_ref[:, c0:c1]
        for t in range(dec_seq):
            gate = z_ref[t, :, off_gate_a + c0:off_gate_a + c1]
            o_ref[t, :, c0:c1] = (mixed[t * gb:(t + 1) * gb] * _silu(gate)).astype(BF16)
    for r in range(n_pst):
        pnew_ref[r] = pool_row(r + dec_seq, 0, d_pool)

    cw = 2 * LANES
    ys = [[] for _ in range(dec_seq)]
    for cb in range(d_conv // cw):
        c0, c1 = cb * cw, (cb + 1) * cw
        a_new = [z_ref[t, :, off_val + c0:off_val + c1] * jax.nn.sigmoid(z_ref[t, :, off_glu + c0:off_glu + c1])
                 for t in range(dec_seq)]

        def conv_row(r, c0=c0, c1=c1, a_new=a_new):
            if r < n_cst:
                return cst_ref[r, :, c0:c1]
            return a_new[r - n_cst]

        acc = [jnp.broadcast_to(bdw_ref[:, c0:c1], (gb, cw)) for _ in range(dec_seq)]
        for r in range(n_cst + dec_seq):
            x_r = conv_row(r)
            for t in range(dec_seq):
                k = r - t
                if 0 <= k < CONV_WIDTH:
                    acc[t] = acc[t] + x_r * wdw_ref[k:k + 1, c0:c1]
            if r >= dec_seq:
                cnew_ref[r - dec_seq, :, c0:c1] = x_r
        for t in range(dec_seq):
            ys[t].append(acc[t])
    y = jnp.concatenate([jnp.concatenate(ys[t], axis=1) for t in range(dec_seq)], axis=0)
    act = _silu(_layernorm(y, lng_ref[...], lnb_ref[...]))
    o_b = jnp.dot(act.astype(BF16), wpw_ref[...], preferred_element_type=F32)
    for t in range(dec_seq):
        gate = z_ref[t, :, off_gate_b:off_gate_b + d_conv]
        o_ref[t, :, d_pool:d_pool + d_conv] = (o_b[t * gb:(t + 1) * gb] * _silu(gate)).astype(BF16)


def _sample_mixer(z3, pstate, cstate, w_pool, pool_scale, w_dw, b_dw, ln_g, ln_b, w_pw, *, gb):
    dec_seq, dec_batch, d_in = z3.shape
    d_pool = pstate.shape[2]
    d_conv = cstate.shape[2]
    assert dec_batch % gb == 0
    kernel = functools.partial(_sample_mixer_kernel, dec_seq=dec_seq, d_pool=d_pool, d_conv=d_conv)
    const2 = lambda i: (0, 0)
    seq_block = lambda a: pl.BlockSpec((a.shape[0], gb, a.shape[2]), lambda i: (0, i, 0))
    return pl.pallas_call(
        kernel,
        grid=(dec_batch // gb,),
        in_specs=[
            seq_block(z3),
            seq_block(pstate),
            seq_block(cstate),
            pl.BlockSpec(w_pool.shape, lambda i: (0, 0, 0)),
            pl.BlockSpec((1, d_pool), const2),
            pl.BlockSpec(w_dw.shape, const2),
            pl.BlockSpec((1, d_conv), const2),
            pl.BlockSpec((1, d_conv), const2),
            pl.BlockSpec((1, d_conv), const2),
            pl.BlockSpec(w_pw.shape, const2),
        ],
        out_specs=[
            pl.BlockSpec((dec_seq, gb, d_pool + d_conv), lambda i: (0, i, 0)),
            seq_block(pstate),
            seq_block(cstate),
        ],
        out_shape=[
            jax.ShapeDtypeStruct((dec_seq, dec_batch, d_pool + d_conv), BF16),
            jax.ShapeDtypeStruct(pstate.shape, F32),
            jax.ShapeDtypeStruct(cstate.shape, F32),
        ],
        compiler_params=_compiler_params(("arbitrary",)),
        name="sample_mixer",
    )(z3, pstate, cstate, w_pool, pool_scale, w_dw, b_dw, ln_g, ln_b, w_pw)


def _sample_attn_kernel(q_ref, gate_ref, k_ref, v_ref, o_ref, *, scale):
    gb, n_rows, _ = k_ref.shape
    n_q = q_ref.shape[1]
    half_rows = n_q // 2
    col = lax.broadcasted_iota(jnp.int32, (n_q, n_rows), 1)
    row = lax.broadcasted_iota(jnp.int32, (n_q, n_rows), 0)
    match = (((col >> 2) & 1) == (row >> 4)) & ((col & 3) == ((row >> 2) & 3))
    valid = match[0:half_rows]
    for b in range(gb):
        k = k_ref[b].astype(BF16)
        v = v_ref[b].astype(BF16)
        p = lax.dot_general(q_ref[b].astype(BF16), k, (((1,), (1,)), ((), ())), preferred_element_type=F32)
        p = jnp.where(match, p, 0.0)
        s = p[0:half_rows] + pltpu.roll(p[half_rows:n_q], n_rows - 4, axis=1)
        s = jnp.where(valid, s * scale, -1e30)
        e = jnp.exp(s - jnp.max(s, axis=-1, keepdims=True))
        e = jnp.where(valid, e, 0.0)
        l = jnp.sum(e, axis=-1, keepdims=True)
        e2 = jnp.concatenate([e, pltpu.roll(e, 4, axis=1)], axis=0)
        o = jnp.dot(e2.astype(BF16), v, preferred_element_type=F32) / jnp.concatenate([l, l], axis=0)
        o_ref[b] = o * _silu(gate_ref[b])


def _sample_attn(qm, gm, cache_k, cache_v, *, gb, xhead_dim):
    nb, n_q, lanes = qm.shape
    n_rows = cache_k.shape[1]
    assert n_q == 2 * N_XHEADS * 4 and N_XHEADS == 4 and lanes == LANES
    kernel = functools.partial(_sample_attn_kernel, scale=xhead_dim ** -0.5)
    small = pl.BlockSpec((gb, n_q, lanes), lambda i: (i, 0, 0))
    big = pl.BlockSpec((gb, n_rows, lanes), lambda i: (i, 0, 0))
    return pl.pallas_call(
        kernel,
        grid=(nb // gb,),
        in_specs=[small, small, big, big],
        out_specs=small,
        out_shape=jax.ShapeDtypeStruct((nb, n_q, lanes), F32),
        compiler_params=_compiler_params(("arbitrary",)),
        name="sample_attn",
    )(qm, gm, cache_k, cache_v)


def _cache_rows(c):
    nb, n_mem, n_heads, e = c.shape
    assert n_heads == N_XHEADS and e == 2 * LANES
    return c.reshape(nb, n_mem, n_heads, 2, LANES).transpose(0, 1, 3, 2, 4).reshape(nb, n_mem * 2 * n_heads, LANES)


def _query_rows(q):
    t, nb, _ = q.shape
    return q.reshape(t, nb, N_XHEADS, 2, LANES).transpose(1, 3, 2, 0, 4).reshape(nb, 2 * N_XHEADS * t, LANES)


def _query_rows_inv(o, t):
    nb = o.shape[0]
    return o.reshape(nb, 2, N_XHEADS, t, LANES).transpose(3, 0, 2, 1, 4).reshape(t, nb, N_XHEADS * 2 * LANES)


def _out_proj_kernel(o_ref, w_ref, x_ref, g_ref, y_ref, acc_ref, ssq_ref, *, n_j, tn, d_model):
    s = pl.program_id(1)

    @pl.when(s < n_j)
    def _():
        part = x_ref[...] + jnp.dot(o_ref[...], w_ref[...], preferred_element_type=F32)
        acc_ref[s] = part
        row_ssq = jnp.sum(part * part, axis=-1, keepdims=True)

        @pl.when(s == 0)
        def _():
            ssq_ref[...] = row_ssq

        @pl.when(s > 0)
        def _():
            ssq_ref[...] += row_ssq

    @pl.when(s >= n_j)
    def _():
        j = s - n_j
        scale = lax.rsqrt(ssq_ref[...] * (1.0 / d_model) + EPS)
        y_ref[...] = acc_ref[j] * scale * g_ref[...]


def _out_proj(o, w, x, g, *, tile):
    tm, tn = tile
    m, k = o.shape
    n = w.shape[1]
    n_j = n // tn
    assert m % tm == 0 and n % tn == 0
    kernel = functools.partial(_out_proj_kernel, n_j=n_j, tn=tn, d_model=n)
    proj_j = lambda s: jnp.minimum(s, n_j - 1)
    return pl.pallas_call(
        kernel,
        grid=(m // tm, 2 * n_j),
        in_specs=[
            pl.BlockSpec((tm, k), lambda i, s: (i, 0)),
            pl.BlockSpec((k, tn), lambda i, s: (0, jnp.where(s < n_j, s, 0))),
            pl.BlockSpec((tm, tn), lambda i, s: (i, proj_j(s))),
            pl.BlockSpec((1, tn), lambda i, s: (0, jnp.maximum(s - n_j, 0))),
        ],
        out_specs=pl.BlockSpec((tm, tn), lambda i, s: (i, jnp.maximum(s - n_j, 0))),
        out_shape=jax.ShapeDtypeStruct((m, n), F32),
        scratch_shapes=[pltpu.VMEM((n_j, tm, tn), F32), pltpu.VMEM((tm, 1), F32)],
        compiler_params=_compiler_params(("arbitrary", "arbitrary")),
        name="out_proj",
    )(o, w, x, g)


def kernel(x_prompt, mem_prompt, x_sample, cache_mem_k, cache_mem_v, state_pool, state_conv, norm_g, mem_norm_g, w_in,
           w_mem_k, w_mem_v, w_pool, pool_scale, w_dw, b_dw, conv_ln_g, conv_ln_b, w_pw, w_out, final_norm_g):
    depth = w_in.shape[0]
    assert depth == 1, "single-layer step"
    batch, seq, d_model = x_prompt.shape
    dec_batch, dec_seq, _ = x_sample.shape
    n_mem = mem_prompt.shape[1]
    d_xattn = w_mem_k.shape[2]
    d_pool = pool_scale.shape[1]
    d_conv = w_pw.shape[1]
    xhead_dim = d_xattn // N_XHEADS
    off_q = 2 * d_pool + 3 * d_conv
    l = 0

    w_in_b = w_in[l].astype(BF16)
    w_out_b = w_out[l].astype(BF16)
    w_mem_b = jnp.concatenate([w_mem_k[l], w_mem_v[l]], axis=1).astype(BF16)
    w_pool_b = w_pool[l].astype(BF16)
    w_pw_b = w_pw[l].astype(BF16)
    g_in = norm_g[l][None, :]
    g_mem = mem_norm_g[l][None, :]
    g_fin = final_norm_g[None, :]
    pscale = pool_scale[l][None, :]
    bdw = b_dw[l][None, :]
    lng = conv_ln_g[l][None, :]
    lnb = conv_ln_b[l][None, :]
    mixer_w = (w_pool_b, pscale, w_dw[l], bdw, lng, lnb, w_pw_b)

    xp = x_prompt.reshape(batch * seq, d_model)
    kv = _norm_matmul(mem_prompt.reshape(batch * n_mem, d_model), g_mem, w_mem_b, tile=FUSED_PROJ_TILE)
    k_p = kv[:, :d_xattn].reshape(batch, n_mem, d_xattn)
    v_p = kv[:, d_xattn:].reshape(batch, n_mem, d_xattn)
    z_p = _matmul(_rmsnorm_bf16(xp, g_in, tm=NORM_ROWS), w_in_b, tile=PROMPT_PROJ_TILE)
    o_p, pool_p, conv_p = _prompt_mixer(z_p, k_p.astype(BF16), v_p.astype(BF16), *mixer_w,
                                        batch=batch, seq=seq, tt=PROMPT_MIXER_ROWS, d_model=d_model)
    y_p = _out_proj(o_p, w_out_b, xp, g_fin, tile=OUT_PROJ_TILE).reshape(batch, seq, d_model)

    xs = x_sample.transpose(1, 0, 2).reshape(dec_seq * dec_batch, d_model)
    z_s = _norm_matmul(xs, g_in, w_in_b, tile=FUSED_PROJ_TILE).reshape(dec_seq, dec_batch, -1)
    o_ab, pool_s, conv_s = _sample_mixer(z_s, state_pool[l].transpose(1, 0, 2), state_conv[l].transpose(1, 0, 2),
                                         *mixer_w, gb=SAMPLE_MIXER_SEQS)
    o_c = _sample_attn(_query_rows(z_s[:, :, off_q:off_q + d_xattn]), _query_rows(z_s[:, :, off_q + d_xattn:]),
                       _cache_rows(cache_mem_k[l]), _cache_rows(cache_mem_v[l]),
                       gb=SAMPLE_ATTN_SEQS, xhead_dim=xhead_dim)
    o_c = _query_rows_inv(o_c, dec_seq)
    o_s = jnp.concatenate([o_ab, o_c.astype(BF16)], axis=-1).reshape(dec_seq * dec_batch, d_model)
    y_s = _out_proj(o_s, w_out_b, xs, g_fin, tile=SAMPLE_OUT_PROJ_TILE)
    y_s = y_s.reshape(dec_seq, dec_batch, d_model).transpose(1, 0, 2)

    mem_shape = (depth, batch, n_mem, N_XHEADS, xhead_dim)
    return (y_p, y_s, k_p.reshape(mem_shape), v_p.reshape(mem_shape), pool_p[None], conv_p[None],
            pool_s.transpose(1, 0, 2)[None], conv_s.transpose(1, 0, 2)[None])
```

```python
import functools

import jax
import jax.numpy as jnp
from jax import lax
from jax.experimental import pallas as pl
from jax.experimental.pallas import tpu as pltpu

EPS = 1e-6
POOL_WINDOWS = (2, 4, 8, 16)
CONV_WIDTH = 31
N_XHEADS = 4
PAST_LEN = 16384

SUBLANES = 8
LANES = 128
VMEM_LIMIT_BYTES = 56 * 1024 * 1024

PROMPT_PROJ_TILE = (1024, 512)
PROMPT_PROJ_NORM_CHUNKS = 16
FUSED_PROJ_TILE = (512, 512)
OUT_PROJ_TILE = (1024, 512)
SAMPLE_OUT_PROJ_TILE = (512, 512)
PROMPT_MIXER_ROWS = 256
POOL_CHUNK = 32
CONV_CHUNK = 64
SAMPLE_MIXER_SEQS = 16
SAMPLE_ATTN_SEQS = 8

BF16 = jnp.bfloat16
F32 = jnp.float32

ACT_NONE, ACT_SILU, ACT_SIGMOID = 0, 1, 2


def _sigmoid(x):
    return 0.5 * jnp.tanh(0.5 * x) + 0.5


def _silu(x):
    return x * _sigmoid(x)


def _activate(r, act):
    if act == ACT_SILU:
        return _silu(r)
    if act == ACT_SIGMOID:
        return _sigmoid(r)
    return r


def _rms_scale(x):
    return lax.rsqrt(jnp.mean(x * x, axis=-1, keepdims=True) + EPS)


def _compiler_params(semantics):
    return pltpu.CompilerParams(dimension_semantics=semantics, vmem_limit_bytes=VMEM_LIMIT_BYTES)


def _act_tile_ranges(segments, tn):
    ranges, off = [], 0
    for width, act in segments:
        assert off % tn == 0 and width % tn == 0, "segments must be whole column tiles"
        if act != ACT_NONE:
            ranges.append((off // tn, (off + width) // tn, act))
        off += width
    return tuple(ranges)


def _tile_act(j, act_ranges):
    act = jnp.int32(ACT_NONE)
    for lo, hi, a in act_ranges:
        act = jnp.where((j >= lo) & (j < hi), a, act)
    return act


def _acts_used(act_ranges):
    return sorted({ACT_NONE} | {a for _, _, a in act_ranges})


def _norm_matmul_kernel(x_ref, g_ref, w_ref, o_ref, h_ref, *, act_ranges):
    j = pl.program_id(1)

    @pl.when(j == 0)
    def _():
        x = x_ref[...]
        h_ref[...] = (x * _rms_scale(x) * g_ref[...]).astype(BF16)

    tile_act = _tile_act(j, act_ranges)
    for act in _acts_used(act_ranges):

        @pl.when(tile_act == act)
        def _(act=act):
            o_ref[...] = _activate(jnp.dot(h_ref[...], w_ref[...], preferred_element_type=F32), act)


def _stream_norm_matmul_kernel(x_ref, g_ref, w_ref, o_ref, h_ref, *, n_j, n_chunk, chunk, act_ranges):
    s = pl.program_id(0)
    step = jnp.maximum(s - n_chunk, 0)
    i = step // n_j
    j = step % n_j

    def normalise_chunk(slot, c):
        x = x_ref[...]
        rows = pl.ds(pl.multiple_of(c * chunk, chunk), chunk)
        h_ref[slot, rows, :] = (x * _rms_scale(x) * g_ref[...]).astype(BF16)

    @pl.when(s < n_chunk)
    def _():
        normalise_chunk(0, s)

    tile_act = _tile_act(j, act_ranges)
    for act in _acts_used(act_ranges):

        @pl.when((s >= n_chunk) & (tile_act == act))
        def _(act=act):
            normalise_chunk((i + 1) % 2, jnp.clip(j - 1, 0, n_chunk - 1))
            o_ref[...] = _activate(jnp.dot(h_ref[i % 2], w_ref[...], preferred_element_type=F32), act)


def _stream_norm_matmul(x, g, w, *, tile, n_chunk, act_ranges=()):
    tm, tn = tile
    m, k = x.shape
    n = w.shape[1]
    n_i, n_j = m // tm, n // tn
    chunk = tm // n_chunk
    assert m % tm == 0 and n % tn == 0 and tm % n_chunk == 0 and n_chunk < n_j

    def steps(s):
        step = jnp.maximum(s - n_chunk, 0)
        return step // n_j, step % n_j

    def x_index(s):
        i, j = steps(s)
        nxt = jnp.minimum((i + 1) * n_chunk + jnp.clip(j - 1, 0, n_chunk - 1), n_i * n_chunk - 1)
        return jnp.where(s < n_chunk, s, nxt), 0

    kernel = functools.partial(_stream_norm_matmul_kernel, n_j=n_j, n_chunk=n_chunk, chunk=chunk,
                               act_ranges=act_ranges)
    return pl.pallas_call(
        kernel,
        grid=(n_chunk + n_i * n_j,),
        in_specs=[
            pl.BlockSpec((chunk, k), x_index),
            pl.BlockSpec((1, k), lambda s: (0, 0)),
            pl.BlockSpec((k, tn), lambda s: (0, steps(s)[1])),
        ],
        out_specs=pl.BlockSpec((tm, tn), lambda s: steps(s)),
        out_shape=jax.ShapeDtypeStruct((m, n), F32),
        scratch_shapes=[pltpu.VMEM((2, tm, k), BF16)],
        compiler_params=_compiler_params(("arbitrary",)),
        name="stream_norm_matmul",
    )(x, g, w)


def _norm_matmul(x, g, w, *, tile, act_ranges=()):
    tm, tn = tile
    m, k = x.shape
    n = w.shape[1]
    assert m % tm == 0 and n % tn == 0
    return pl.pallas_call(
        functools.partial(_norm_matmul_kernel, act_ranges=act_ranges),
        grid=(m // tm, n // tn),
        in_specs=[
            pl.BlockSpec((tm, k), lambda i, j: (i, 0)),
            pl.BlockSpec((1, k), lambda i, j: (0, 0)),
            pl.BlockSpec((k, tn), lambda i, j: (0, j)),
        ],
        out_specs=pl.BlockSpec((tm, tn), lambda i, j: (i, j)),
        out_shape=jax.ShapeDtypeStruct((m, n), F32),
        scratch_shapes=[pltpu.VMEM((tm, k), BF16)],
        compiler_params=_compiler_params(("arbitrary", "arbitrary")),
        name="norm_matmul",
    )(x, g, w)


POOL_HALO = 16
CONV_HALO = 32


def _pool_tile(extp_ref, pooled_ref, tt, d_pool, t_idx):
    pg = d_pool // len(POOL_WINDOWS)
    row = lax.broadcasted_iota(jnp.int32, (POOL_CHUNK, LANES), 0)

    def body(i, carry):
        base = pl.multiple_of(i * POOL_CHUNK, POOL_CHUNK)
        n_prev = t_idx * tt + base + row + 1
        for gi, w in enumerate(POOL_WINDOWS):
            inv_cnt = 1.0 / jnp.minimum(w, n_prev).astype(F32)
            for cb in range(gi * pg // LANES, (gi + 1) * pg // LANES):
                cur = extp_ref[cb, pl.ds(base + POOL_HALO, POOL_CHUNK), :]
                s = cur
                for k in range(1, w):
                    s = s + extp_ref[cb, pl.ds(base + POOL_HALO - k, POOL_CHUNK), :]
                pooled_ref[pl.ds(base, POOL_CHUNK), cb * LANES:(cb + 1) * LANES] = s * inv_cnt - cur
        return carry

    lax.fori_loop(0, tt // POOL_CHUNK, body, 0)


def _dwconv_tile(extc_ref, wdw_ref, bdw_ref, y_ref, tt, d_conv):
    first = CONV_HALO - (CONV_WIDTH - 1)
    for cb in range(d_conv // LANES):
        c0, c1 = cb * LANES, (cb + 1) * LANES

        def body(i, carry, cb=cb, c0=c0, c1=c1):
            base = pl.multiple_of(i * CONV_CHUNK, CONV_CHUNK)
            acc = jnp.broadcast_to(bdw_ref[:, c0:c1], (CONV_CHUNK, LANES))
            for k in range(CONV_WIDTH):
                acc = acc + extc_ref[cb, pl.ds(base + first + k, CONV_CHUNK), :] * wdw_ref[k:k + 1, c0:c1]
            y_ref[pl.ds(base, CONV_CHUNK), c0:c1] = acc
            return carry

        lax.fori_loop(0, tt // CONV_CHUNK, body, 0)


def _layernorm(y, g, b):
    mu = jnp.mean(y, axis=-1, keepdims=True)
    d = y - mu
    var = jnp.mean(d * d, axis=-1, keepdims=True)
    return d * lax.rsqrt(var + EPS) * g + b


def _attend(q, k, v, xhead_dim):
    s = lax.dot_general(q.astype(BF16), k, (((1,), (1,)), ((), ())), preferred_element_type=F32)
    s = s * (xhead_dim ** -0.5)
    e = jnp.exp(s - jnp.max(s, axis=-1, keepdims=True))
    l = jnp.sum(e, axis=-1, keepdims=True)
    return jnp.dot(e.astype(BF16), v, preferred_element_type=F32) / l


def _prompt_mixer_kernel(z_ref, k_ref, v_ref, wpool_ref, pscale_ref, wdw_ref, bdw_ref, lng_ref, lnb_ref, wpw_ref,
                         o_ref, pstate_ref, cstate_ref, extp_ref, extc_ref, pooled_ref, y_ref,
                         *, tt, d_pool, d_conv, d_xattn):
    t_idx = pl.program_id(1)
    n_t = pl.num_programs(1)
    pg = d_pool // len(POOL_WINDOWS)
    xhead_dim = d_xattn // N_XHEADS
    off_gate_a = d_pool
    off_val = 2 * d_pool
    off_glu = off_val + d_conv
    off_gate_b = off_glu + d_conv
    off_q = off_gate_b + d_conv
    off_gate_c = off_q + d_xattn

    @pl.when(t_idx == 0)
    def _():
        extp_ref[:, 0:POOL_HALO, :] = jnp.zeros((d_pool // LANES, POOL_HALO, LANES), F32)
        extc_ref[:, 0:CONV_HALO, :] = jnp.zeros((d_conv // LANES, CONV_HALO, LANES), F32)

    for cb in range(d_pool // LANES):
        extp_ref[cb, POOL_HALO:POOL_HALO + tt, :] = z_ref[:, cb * LANES:(cb + 1) * LANES]
    _pool_tile(extp_ref, pooled_ref, tt, d_pool, t_idx)
    for gi in range(len(POOL_WINDOWS)):
        c0, c1 = gi * pg, (gi + 1) * pg
        mixed = jnp.dot(pooled_ref[:, c0:c1].astype(BF16), wpool_ref[gi], preferred_element_type=F32)
        mixed = mixed * pscale_ref[:, c0:c1]
        o_ref[:, c0:c1] = (mixed * z_ref[:, off_gate_a + c0:off_gate_a + c1]).astype(BF16)

    for cb in range(d_conv // LANES):
        c0, c1 = cb * LANES, (cb + 1) * LANES
        extc_ref[cb, CONV_HALO:CONV_HALO + tt, :] = z_ref[:, off_val + c0:off_val + c1] * z_ref[:, off_glu + c0:off_glu + c1]
    _dwconv_tile(extc_ref, wdw_ref, bdw_ref, y_ref, tt, d_conv)
    act = _silu(_layernorm(y_ref[...], lng_ref[...], lnb_ref[...]))
    o_b = jnp.dot(act.astype(BF16), wpw_ref[...], preferred_element_type=F32)
    o_ref[:, d_pool:d_pool + d_conv] = (o_b * z_ref[:, off_gate_b:off_gate_b + d_conv]).astype(BF16)

    for h in range(N_XHEADS):
        c0, c1 = h * xhead_dim, (h + 1) * xhead_dim
        o_c = _attend(z_ref[:, off_q + c0:off_q + c1], k_ref[0, :, c0:c1], v_ref[0, :, c0:c1], xhead_dim)
        o_c = o_c * z_ref[:, off_gate_c + c0:off_gate_c + c1]
        o_ref[:, d_pool + d_conv + c0:d_pool + d_conv + c1] = o_c.astype(BF16)

    @pl.when(t_idx == n_t - 1)
    def _():
        for cb in range(d_pool // LANES):
            pstate_ref[0, :, cb * LANES:(cb + 1) * LANES] = extp_ref[cb, tt + 1:tt + POOL_HALO, :]
        for cb in range(d_conv // LANES):
            cstate_ref[0, :, cb * LANES:(cb + 1) * LANES] = extc_ref[cb, tt + 2:tt + CONV_HALO, :]

    extp_ref[:, 0:POOL_HALO, :] = extp_ref[:, tt:tt + POOL_HALO, :]
    extc_ref[:, 0:CONV_HALO, :] = extc_ref[:, tt:tt + CONV_HALO, :]


def _prompt_mixer(z, k, v, w_pool, pool_scale, w_dw, b_dw, ln_g, ln_b, w_pw, *, batch, seq, tt, d_model):
    d_in = z.shape[1]
    d_pool = w_pool.shape[0] * w_pool.shape[1]
    d_conv = w_pw.shape[0]
    d_xattn = k.shape[2]
    n_mem = k.shape[1]
    n_t = seq // tt
    assert seq % tt == 0 and tt % CONV_CHUNK == 0 and tt % POOL_CHUNK == 0
    kernel = functools.partial(_prompt_mixer_kernel, tt=tt, d_pool=d_pool, d_conv=d_conv, d_xattn=d_xattn)
    const2 = lambda b, t: (0, 0)
    return pl.pallas_call(
        kernel,
        grid=(batch, n_t),
        in_specs=[
            pl.BlockSpec((tt, d_in), lambda b, t: (b * n_t + t, 0)),
            pl.BlockSpec((1, n_mem, d_xattn), lambda b, t: (b, 0, 0)),
            pl.BlockSpec((1, n_mem, d_xattn), lambda b, t: (b, 0, 0)),
            pl.BlockSpec(w_pool.shape, lambda b, t: (0, 0, 0)),
            pl.BlockSpec((1, d_pool), const2),
            pl.BlockSpec(w_dw.shape, const2),
            pl.BlockSpec((1, d_conv), const2),
            pl.BlockSpec((1, d_conv), const2),
            pl.BlockSpec((1, d_conv), const2),
            pl.BlockSpec(w_pw.shape, const2),
        ],
        out_specs=[
            pl.BlockSpec((tt, d_model), lambda b, t: (b * n_t + t, 0)),
            pl.BlockSpec((1, POOL_HALO - 1, d_pool), lambda b, t: (b, 0, 0)),
            pl.BlockSpec((1, CONV_HALO - 2, d_conv), lambda b, t: (b, 0, 0)),
        ],
        out_shape=[
            jax.ShapeDtypeStruct((batch * seq, d_model), BF16),
            jax.ShapeDtypeStruct((batch, POOL_HALO - 1, d_pool), F32),
            jax.ShapeDtypeStruct((batch, CONV_HALO - 2, d_conv), F32),
        ],
        scratch_shapes=[
            pltpu.VMEM((d_pool // LANES, POOL_HALO + tt, LANES), F32),
            pltpu.VMEM((d_conv // LANES, CONV_HALO + tt, LANES), F32),
            pltpu.VMEM((tt, d_pool), F32),
            pltpu.VMEM((tt, d_conv), F32),
        ],
        compiler_params=_compiler_params(("arbitrary", "arbitrary")),
        name="prompt_mixer",
    )(z, k, v, w_pool, pool_scale, w_dw, b_dw, ln_g, ln_b, w_pw)


def _sample_mixer_kernel(z_ref, pst_ref, cst_ref, wpool_ref, pscale_ref, wdw_ref, bdw_ref, lng_ref, lnb_ref, wpw_ref,
                         o_ref, pnew_ref, cnew_ref, *, dec_seq, d_pool, d_conv):
    pg = d_pool // len(POOL_WINDOWS)
    n_pst = pst_ref.shape[0]
    n_cst = cst_ref.shape[0]
    gb = z_ref.shape[1]
    off_gate_a = d_pool
    off_val = 2 * d_pool
    off_glu = off_val + d_conv
    off_gate_b = off_glu + d_conv

    def pool_row(r, c0, c1):
        if r < n_pst:
            return pst_ref[r, :, c0:c1]
        return z_ref[r - n_pst, :, c0:c1]

    for gi, w in enumerate(POOL_WINDOWS):
        c0, c1 = gi * pg, (gi + 1) * pg
        pooled = []
        for t in range(dec_seq):
            s = pool_row(n_pst + t, c0, c1)
            cur = s
            for k in range(1, w):
                s = s + pool_row(n_pst + t - k, c0, c1)
            cnt = float(min(w, PAST_LEN + t + 1))
            pooled.append((s / cnt - cur).astype(BF16))
        mixed = jnp.dot(jnp.concatenate(pooled, axis=0), wpool_ref[gi], preferred_element_type=F32)
        mixed = mixed * pscale_ref[:, c0:c1]
        for t in range(dec_seq):
            gate = z_ref[t, :, off_gate_a + c0:off_gate_a + c1]
            o_ref[t, :, c0:c1] = (mixed[t * gb:(t + 1) * gb] * gate).astype(BF16)
    for r in range(n_pst):
        pnew_ref[r] = pool_row(r + dec_seq, 0, d_pool)

    cw = 2 * LANES
    ys = [[] for _ in range(dec_seq)]
    for cb in range(d_conv // cw):
        c0, c1 = cb * cw, (cb + 1) * cw
        a_new = [z_ref[t, :, off_val + c0:off_val + c1] * z_ref[t, :, off_glu + c0:off_glu + c1]
                 for t in range(dec_seq)]

        def conv_row(r, c0=c0, c1=c1, a_new=a_new):
            if r < n_cst:
                return cst_ref[r, :, c0:c1]
            return a_new[r - n_cst]

        acc = [jnp.broadcast_to(bdw_ref[:, c0:c1], (gb, cw)) for _ in range(dec_seq)]
        for r in range(n_cst + dec_seq):
            x_r = conv_row(r)
            for t in range(dec_seq):
                k = r - t
                if 0 <= k < CONV_WIDTH:
                    acc[t] = acc[t] + x_r * wdw_ref[k:k + 1, c0:c1]
            if r >= dec_seq:
                cnew_ref[r - dec_seq, :, c0:c1] = x_r
        for t in range(dec_seq):
            ys[t].append(acc[t])
    y = jnp.concatenate([jnp.concatenate(ys[t], axis=1) for t in range(dec_seq)], axis=0)
    act = _silu(_layernorm(y, lng_ref[...], lnb_ref[...]))
    o_b = jnp.dot(act.astype(BF16), wpw_ref[...], preferred_element_type=F32)
    for t in range(dec_seq):
        gate = z_ref[t, :, off_gate_b:off_gate_b + d_conv]
        o_ref[t, :, d_pool:d_pool + d_conv] = (o_b[t * gb:(t + 1) * gb] * gate).astype(BF16)


def _sample_mixer(z3, pstate, cstate, w_pool, pool_scale, w_dw, b_dw, ln_g, ln_b, w_pw, *, gb):
    dec_seq, dec_batch, d_in = z3.shape
    d_pool = pstate.shape[2]
    d_conv = cstate.shape[2]
    assert dec_batch % gb == 0
    kernel = functools.partial(_sample_mixer_kernel, dec_seq=dec_seq, d_pool=d_pool, d_conv=d_conv)
    const2 = lambda i: (0, 0)
    seq_block = lambda a: pl.BlockSpec((a.shape[0], gb, a.shape[2]), lambda i: (0, i, 0))
    return pl.pallas_call(
        kernel,
        grid=(dec_batch // gb,),
        in_specs=[
            seq_block(z3),
            seq_block(pstate),
            seq_block(cstate),
            pl.BlockSpec(w_pool.shape, lambda i: (0, 0, 0)),
            pl.BlockSpec((1, d_pool), const2),
            pl.BlockSpec(w_dw.shape, const2),
            pl.BlockSpec((1, d_conv), const2),
            pl.BlockSpec((1, d_conv), const2),
            pl.BlockSpec((1, d_conv), const2),
            pl.BlockSpec(w_pw.shape, const2),
        ],
        out_specs=[
            pl.BlockSpec((dec_seq, gb, d_pool + d_conv), lambda i: (0, i, 0)),
            seq_block(pstate),
            seq_block(cstate),
        ],
        out_shape=[
            jax.ShapeDtypeStruct((dec_seq, dec_batch, d_pool + d_conv), BF16),
            jax.ShapeDtypeStruct(pstate.shape, F32),
            jax.ShapeDtypeStruct(cstate.shape, F32),
        ],
        compiler_params=_compiler_params(("arbitrary",)),
        name="sample_mixer",
    )(z3, pstate, cstate, w_pool, pool_scale, w_dw, b_dw, ln_g, ln_b, w_pw)


def _sample_attn_kernel(q_ref, gate_ref, k_ref, v_ref, o_ref, *, scale):
    gb, n_rows, _ = k_ref.shape
    n_q = q_ref.shape[1]
    half_rows = n_q // 2
    col = lax.broadcasted_iota(jnp.int32, (n_q, n_rows), 1)
    row = lax.broadcasted_iota(jnp.int32, (n_q, n_rows), 0)
    match = (((col >> 2) & 1) == (row >> 4)) & ((col & 3) == ((row >> 2) & 3))
    valid = match[0:half_rows]
    for b in range(gb):
        k = k_ref[b].astype(BF16)
        v = v_ref[b].astype(BF16)
        p = lax.dot_general(q_ref[b].astype(BF16), k, (((1,), (1,)), ((), ())), preferred_element_type=F32)
        p = jnp.where(match, p, 0.0)
        s = p[0:half_rows] + pltpu.roll(p[half_rows:n_q], n_rows - 4, axis=1)
        s = jnp.where(valid, s * scale, -1e30)
        e = jnp.exp(s - jnp.max(s, axis=-1, keepdims=True))
        e = jnp.where(valid, e, 0.0)
        l = jnp.sum(e, axis=-1, keepdims=True)
        e2 = jnp.concatenate([e, pltpu.roll(e, 4, axis=1)], axis=0)
        o = jnp.dot(e2.astype(BF16), v, preferred_element_type=F32) / jnp.concatenate([l, l], axis=0)
        o_ref[b] = o * gate_ref[b]


def _sample_attn(qm, gm, cache_k, cache_v, *, gb, xhead_dim):
    nb, n_q, lanes = qm.shape
    n_rows = cache_k.shape[1]
    assert n_q == 2 * N_XHEADS * 4 and N_XHEADS == 4 and lanes == LANES
    kernel = functools.partial(_sample_attn_kernel, scale=xhead_dim ** -0.5)
    small = pl.BlockSpec((gb, n_q, lanes), lambda i: (i, 0, 0))
    big = pl.BlockSpec((gb, n_rows, lanes), lambda i: (i, 0, 0))
    return pl.pallas_call(
        kernel,
        grid=(nb // gb,),
        in_specs=[small, small, big, big],
        out_specs=small,
        out_shape=jax.ShapeDtypeStruct((nb, n_q, lanes), F32),
        compiler_params=_compiler_params(("arbitrary",)),
        name="sample_attn",
    )(qm, gm, cache_k, cache_v)


def _cache_rows(c):
    nb, n_mem, n_heads, e = c.shape
    assert n_heads == N_XHEADS and e == 2 * LANES
    return c.reshape(nb, n_mem, n_heads, 2, LANES).transpose(0, 1, 3, 2, 4).reshape(nb, n_mem * 2 * n_heads, LANES)


def _query_rows(q):
    t, nb, _ = q.shape
    return q.reshape(t, nb, N_XHEADS, 2, LANES).transpose(1, 3, 2, 0, 4).reshape(nb, 2 * N_XHEADS * t, LANES)


def _query_rows_inv(o, t):
    nb = o.shape[0]
    return o.reshape(nb, 2, N_XHEADS, t, LANES).transpose(3, 0, 2, 1, 4).reshape(t, nb, N_XHEADS * 2 * LANES)


def _out_proj_kernel(o_ref, w_ref, x_ref, g_ref, y_ref, acc_ref, ssq_ref, *, n_j, tn, d_model):
    s = pl.program_id(1)

    @pl.when(s < n_j)
    def _():
        part = x_ref[...] + jnp.dot(o_ref[...], w_ref[...], preferred_element_type=F32)
        acc_ref[s] = part
        row_ssq = jnp.sum(part * part, axis=-1, keepdims=True)

        @pl.when(s == 0)
        def _():
            ssq_ref[...] = row_ssq

        @pl.when(s > 0)
        def _():
            ssq_ref[...] += row_ssq

    @pl.when(s >= n_j)
    def _():
        j = s - n_j
        scale = lax.rsqrt(ssq_ref[...] * (1.0 / d_model) + EPS)
        y_ref[...] = acc_ref[j] * scale * g_ref[...]


def _out_proj(o, w, x, g, *, tile):
    tm, tn = tile
    m, k = o.shape
    n = w.shape[1]
    n_j = n // tn
    assert m % tm == 0 and n % tn == 0
    n_i = m // tm
    kernel = functools.partial(_out_proj_kernel, n_j=n_j, tn=tn, d_model=n)
    row_tile = lambda i, s: jnp.minimum(i + (s >= n_j).astype(jnp.int32), n_i - 1)
    proj_j = lambda s: jnp.where(s < n_j, s, 0)
    return pl.pallas_call(
        kernel,
        grid=(n_i, 2 * n_j),
        in_specs=[
            pl.BlockSpec((tm, k), lambda i, s: (row_tile(i, s), 0)),
            pl.BlockSpec((k, tn), lambda i, s: (0, proj_j(s))),
            pl.BlockSpec((tm, tn), lambda i, s: (row_tile(i, s), proj_j(s))),
            pl.BlockSpec((1, tn), lambda i, s: (0, jnp.maximum(s - n_j, 0))),
        ],
        out_specs=pl.BlockSpec((tm, tn), lambda i, s: (i, jnp.maximum(s - n_j, 0))),
        out_shape=jax.ShapeDtypeStruct((m, n), F32),
        scratch_shapes=[pltpu.VMEM((n_j, tm, tn), F32), pltpu.VMEM((tm, 1), F32)],
        compiler_params=_compiler_params(("arbitrary", "arbitrary")),
        name="out_proj",
    )(o, w, x, g)


def kernel(x_prompt, mem_prompt, x_sample, cache_mem_k, cache_mem_v, state_pool, state_conv, norm_g, mem_norm_g, w_in,
           w_mem_k, w_mem_v, w_pool, pool_scale, w_dw, b_dw, conv_ln_g, conv_ln_b, w_pw, w_out, final_norm_g):
    depth = w_in.shape[0]
    assert depth == 1, "single-layer step"
    batch, seq, d_model = x_prompt.shape
    dec_batch, dec_seq, _ = x_sample.shape
    n_mem = mem_prompt.shape[1]
    d_xattn = w_mem_k.shape[2]
    d_pool = pool_scale.shape[1]
    d_conv = w_pw.shape[1]
    xhead_dim = d_xattn // N_XHEADS
    off_q = 2 * d_pool + 3 * d_conv
    l = 0

    w_in_b = w_in[l].astype(BF16)
    w_out_b = w_out[l].astype(BF16)
    w_mem_b = jnp.concatenate([w_mem_k[l], w_mem_v[l]], axis=1).astype(BF16)
    w_pool_b = w_pool[l].astype(BF16)
    w_pw_b = w_pw[l].astype(BF16)
    g_in = norm_g[l][None, :]
    g_mem = mem_norm_g[l][None, :]
    g_fin = final_norm_g[None, :]
    pscale = pool_scale[l][None, :]
    bdw = b_dw[l][None, :]
    lng = conv_ln_g[l][None, :]
    lnb = conv_ln_b[l][None, :]
    mixer_w = (w_pool_b, pscale, w_dw[l], bdw, lng, lnb, w_pw_b)
    z_segments = ((d_pool, ACT_NONE), (d_pool, ACT_SILU), (d_conv, ACT_NONE), (d_conv, ACT_SIGMOID),
                  (d_conv, ACT_SILU), (d_xattn, ACT_NONE), (d_xattn, ACT_SILU))

    xp = x_prompt.reshape(batch * seq, d_model)
    kv = _norm_matmul(mem_prompt.reshape(batch * n_mem, d_model), g_mem, w_mem_b, tile=FUSED_PROJ_TILE)
    k_p = kv[:, :d_xattn].reshape(batch, n_mem, d_xattn)
    v_p = kv[:, d_xattn:].reshape(batch, n_mem, d_xattn)
    z_p = _stream_norm_matmul(xp, g_in, w_in_b, tile=PROMPT_PROJ_TILE, n_chunk=PROMPT_PROJ_NORM_CHUNKS,
                              act_ranges=_act_tile_ranges(z_segments, PROMPT_PROJ_TILE[1]))
    o_p, pool_p, conv_p = _prompt_mixer(z_p, k_p.astype(BF16), v_p.astype(BF16), *mixer_w,
                                        batch=batch, seq=seq, tt=PROMPT_MIXER_ROWS, d_model=d_model)
    y_p = _out_proj(o_p, w_out_b, xp, g_fin, tile=OUT_PROJ_TILE).reshape(batch, seq, d_model)

    xs = x_sample.transpose(1, 0, 2).reshape(dec_seq * dec_batch, d_model)
    z_s = _norm_matmul(xs, g_in, w_in_b, tile=FUSED_PROJ_TILE,
                       act_ranges=_act_tile_ranges(z_segments, FUSED_PROJ_TILE[1])).reshape(dec_seq, dec_batch, -1)
    o_ab, pool_s, conv_s = _sample_mixer(z_s, state_pool[l].transpose(1, 0, 2), state_conv[l].transpose(1, 0, 2),
                                         *mixer_w, gb=SAMPLE_MIXER_SEQS)
    o_c = _sample_attn(_query_rows(z_s[:, :, off_q:off_q + d_xattn]), _query_rows(z_s[:, :, off_q + d_xattn:]),
                       _cache_rows(cache_mem_k[l]), _cache_rows(cache_mem_v[l]),
                       gb=SAMPLE_ATTN_SEQS, xhead_dim=xhead_dim)
    o_c = _query_rows_inv(o_c, dec_seq)
    o_s = jnp.concatenate([o_ab, o_c.astype(BF16)], axis=-1).reshape(dec_seq * dec_batch, d_model)
    y_s = _out_proj(o_s, w_out_b, xs, g_fin, tile=SAMPLE_OUT_PROJ_TILE)
    y_s = y_s.reshape(dec_seq, dec_batch, d_model).transpose(1, 0, 2)

    mem_shape = (depth, batch, n_mem, N_XHEADS, xhead_dim)
    return (y_p, y_s, k_p.reshape(mem_shape), v_p.reshape(mem_shape), pool_p[None], conv_p[None],
            pool_s.transpose(1, 0, 2)[None], conv_s.transpose(1, 0, 2)[None])
```

```python
import functools

import jax
import jax.numpy as jnp
from jax import lax
from jax.experimental import pallas as pl
from jax.experimental.pallas import tpu as pltpu

EPS = 1e-6
POOL_WINDOWS = (2, 4, 8, 16)
CONV_WIDTH = 31
N_XHEADS = 4
PAST_LEN = 16384

SUBLANES = 8
LANES = 128
VMEM_LIMIT_BYTES = 56 * 1024 * 1024

PROMPT_PROJ_TILE = (1024, 512)
PROMPT_PROJ_NORM_CHUNKS = 16
FUSED_PROJ_TILE = (512, 512)
OUT_PROJ_TILE = (1024, 512)
SAMPLE_OUT_PROJ_TILE = (512, 512)
PROMPT_MIXER_ROWS = 256
POOL_CHUNK = 32
CONV_CHUNK = 64
SAMPLE_MIXER_SEQS = 16
SAMPLE_ATTN_SEQS = 8

BF16 = jnp.bfloat16
F32 = jnp.float32

ACT_NONE, ACT_SILU, ACT_SIGMOID = 0, 1, 2


def _sigmoid(x):
    return 0.5 * jnp.tanh(0.5 * x) + 0.5


def _silu(x):
    return x * _sigmoid(x)


def _activate(r, act):
    if act == ACT_SILU:
        return _silu(r)
    if act == ACT_SIGMOID:
        return _sigmoid(r)
    return r


def _rms_scale(x):
    return lax.rsqrt(jnp.mean(x * x, axis=-1, keepdims=True) + EPS)


def _compiler_params(semantics):
    return pltpu.CompilerParams(dimension_semantics=semantics, vmem_limit_bytes=VMEM_LIMIT_BYTES)


def _act_tile_ranges(segments, tn):
    ranges, off = [], 0
    for width, act in segments:
        assert off % tn == 0 and width % tn == 0, "segments must be whole column tiles"
        if act != ACT_NONE:
            ranges.append((off // tn, (off + width) // tn, act))
        off += width
    return tuple(ranges)


def _tile_act(j, act_ranges):
    act = jnp.int32(ACT_NONE)
    for lo, hi, a in act_ranges:
        act = jnp.where((j >= lo) & (j < hi), a, act)
    return act


def _acts_used(act_ranges):
    return sorted({ACT_NONE} | {a for _, _, a in act_ranges})


def _norm_matmul_kernel(x_ref, g_ref, w_ref, o_ref, h_ref, *, act_ranges):
    j = pl.program_id(1)

    @pl.when(j == 0)
    def _():
        x = x_ref[...]
        h_ref[...] = (x * _rms_scale(x) * g_ref[...]).astype(BF16)

    tile_act = _tile_act(j, act_ranges)
    for act in _acts_used(act_ranges):

        @pl.when(tile_act == act)
        def _(act=act):
            o_ref[...] = _activate(jnp.dot(h_ref[...], w_ref[...].astype(BF16), preferred_element_type=F32), act)


def _stream_norm_matmul_kernel(x_ref, g_ref, w_ref, o_ref, h_ref, *, n_j, n_chunk, chunk, act_ranges):
    s = pl.program_id(0)
    step = jnp.maximum(s - n_chunk, 0)
    i = step // n_j
    j = step % n_j

    def normalise_chunk(slot, c):
        x = x_ref[...]
        rows = pl.ds(pl.multiple_of(c * chunk, chunk), chunk)
        h_ref[slot, rows, :] = (x * _rms_scale(x) * g_ref[...]).astype(BF16)

    @pl.when(s < n_chunk)
    def _():
        normalise_chunk(0, s)

    tile_act = _tile_act(j, act_ranges)
    for act in _acts_used(act_ranges):

        @pl.when((s >= n_chunk) & (tile_act == act))
        def _(act=act):
            normalise_chunk((i + 1) % 2, jnp.clip(j - 1, 0, n_chunk - 1))
            o_ref[...] = _activate(jnp.dot(h_ref[i % 2], w_ref[...].astype(BF16), preferred_element_type=F32), act)


def _stream_norm_matmul(x, g, w, *, tile, n_chunk, act_ranges=()):
    tm, tn = tile
    m, k = x.shape
    n = w.shape[1]
    n_i, n_j = m // tm, n // tn
    chunk = tm // n_chunk
    assert m % tm == 0 and n % tn == 0 and tm % n_chunk == 0 and n_chunk < n_j

    def steps(s):
        step = jnp.maximum(s - n_chunk, 0)
        return step // n_j, step % n_j

    def x_index(s):
        i, j = steps(s)
        nxt = jnp.minimum((i + 1) * n_chunk + jnp.clip(j - 1, 0, n_chunk - 1), n_i * n_chunk - 1)
        return jnp.where(s < n_chunk, s, nxt), 0

    kernel = functools.partial(_stream_norm_matmul_kernel, n_j=n_j, n_chunk=n_chunk, chunk=chunk,
                               act_ranges=act_ranges)
    return pl.pallas_call(
        kernel,
        grid=(n_chunk + n_i * n_j,),
        in_specs=[
            pl.BlockSpec((chunk, k), x_index),
            pl.BlockSpec((1, k), lambda s: (0, 0)),
            pl.BlockSpec((k, tn), lambda s: (0, steps(s)[1])),
        ],
        out_specs=pl.BlockSpec((tm, tn), lambda s: steps(s)),
        out_shape=jax.ShapeDtypeStruct((m, n), F32),
        scratch_shapes=[pltpu.VMEM((2, tm, k), BF16)],
        compiler_params=_compiler_params(("arbitrary",)),
        name="stream_norm_matmul",
    )(x, g, w)


def _norm_matmul(x, g, w, *, tile, act_ranges=()):
    tm, tn = tile
    m, k = x.shape
    n = w.shape[1]
    assert m % tm == 0 and n % tn == 0
    return pl.pallas_call(
        functools.partial(_norm_matmul_kernel, act_ranges=act_ranges),
        grid=(m // tm, n // tn),
        in_specs=[
            pl.BlockSpec((tm, k), lambda i, j: (i, 0)),
            pl.BlockSpec((1, k), lambda i, j: (0, 0)),
            pl.BlockSpec((k, tn), lambda i, j: (0, j)),
        ],
        out_specs=pl.BlockSpec((tm, tn), lambda i, j: (i, j)),
        out_shape=jax.ShapeDtypeStruct((m, n), F32),
        scratch_shapes=[pltpu.VMEM((tm, k), BF16)],
        compiler_params=_compiler_params(("arbitrary", "arbitrary")),
        name="norm_matmul",
    )(x, g, w)


POOL_HALO = 16
CONV_HALO = 32


def _pool_tile(extp_ref, pooled_ref, tt, d_pool, t_idx):
    pg = d_pool // len(POOL_WINDOWS)
    row = lax.broadcasted_iota(jnp.int32, (POOL_CHUNK, LANES), 0)

    def body(i, carry):
        base = pl.multiple_of(i * POOL_CHUNK, POOL_CHUNK)
        n_prev = t_idx * tt + base + row + 1
        for gi, w in enumerate(POOL_WINDOWS):
            inv_cnt = 1.0 / jnp.minimum(w, n_prev).astype(F32)
            for cb in range(gi * pg // LANES, (gi + 1) * pg // LANES):
                cur = extp_ref[cb, pl.ds(base + POOL_HALO, POOL_CHUNK), :]
                s = cur
                for k in range(1, w):
                    s = s + extp_ref[cb, pl.ds(base + POOL_HALO - k, POOL_CHUNK), :]
                pooled_ref[pl.ds(base, POOL_CHUNK), cb * LANES:(cb + 1) * LANES] = s * inv_cnt - cur
        return carry

    lax.fori_loop(0, tt // POOL_CHUNK, body, 0)


def _dwconv_tile(extc_ref, wdw_ref, bdw_ref, y_ref, tt, d_conv):
    first = CONV_HALO - (CONV_WIDTH - 1)
    for cb in range(d_conv // LANES):
        c0, c1 = cb * LANES, (cb + 1) * LANES
        for base in range(0, tt, CONV_CHUNK):
            acc = jnp.broadcast_to(bdw_ref[:, c0:c1], (CONV_CHUNK, LANES))
            for k in range(CONV_WIDTH):
                acc = acc + extc_ref[cb, base + first + k:base + first + k + CONV_CHUNK, :] * wdw_ref[k:k + 1, c0:c1]
            y_ref[base:base + CONV_CHUNK, c0:c1] = acc


def _layernorm(y, g, b):
    mu = jnp.mean(y, axis=-1, keepdims=True)
    d = y - mu
    var = jnp.mean(d * d, axis=-1, keepdims=True)
    return d * lax.rsqrt(var + EPS) * g + b


def _attend(q, k, v, xhead_dim):
    s = lax.dot_general(q.astype(BF16), k, (((1,), (1,)), ((), ())), preferred_element_type=F32)
    s = s * (xhead_dim ** -0.5)
    e = jnp.exp(s - jnp.max(s, axis=-1, keepdims=True))
    l = jnp.sum(e, axis=-1, keepdims=True)
    return jnp.dot(e.astype(BF16), v, preferred_element_type=F32) / l


def _prompt_mixer_kernel(z_ref, k_ref, v_ref, wpool_ref, pscale_ref, wdw_ref, bdw_ref, lng_ref, lnb_ref, wpw_ref,
                         o_ref, pstate_ref, cstate_ref, extp_ref, extc_ref, pooled_ref, y_ref,
                         *, tt, d_pool, d_conv, d_xattn):
    t_idx = pl.program_id(1)
    n_t = pl.num_programs(1)
    pg = d_pool // len(POOL_WINDOWS)
    xhead_dim = d_xattn // N_XHEADS
    off_gate_a = d_pool
    off_val = 2 * d_pool
    off_glu = off_val + d_conv
    off_gate_b = off_glu + d_conv
    off_q = off_gate_b + d_conv
    off_gate_c = off_q + d_xattn

    @pl.when(t_idx == 0)
    def _():
        extp_ref[:, 0:POOL_HALO, :] = jnp.zeros((d_pool // LANES, POOL_HALO, LANES), F32)
        extc_ref[:, 0:CONV_HALO, :] = jnp.zeros((d_conv // LANES, CONV_HALO, LANES), F32)

    for cb in range(d_pool // LANES):
        extp_ref[cb, POOL_HALO:POOL_HALO + tt, :] = z_ref[:, cb * LANES:(cb + 1) * LANES]
    _pool_tile(extp_ref, pooled_ref, tt, d_pool, t_idx)
    for gi in range(len(POOL_WINDOWS)):
        c0, c1 = gi * pg, (gi + 1) * pg
        mixed = jnp.dot(pooled_ref[:, c0:c1].astype(BF16), wpool_ref[gi], preferred_element_type=F32)
        mixed = mixed * pscale_ref[:, c0:c1]
        o_ref[:, c0:c1] = (mixed * z_ref[:, off_gate_a + c0:off_gate_a + c1]).astype(BF16)

    for cb in range(d_conv // LANES):
        c0, c1 = cb * LANES, (cb + 1) * LANES
        extc_ref[cb, CONV_HALO:CONV_HALO + tt, :] = z_ref[:, off_val + c0:off_val + c1] * z_ref[:, off_glu + c0:off_glu + c1]
    _dwconv_tile(extc_ref, wdw_ref, bdw_ref, y_ref, tt, d_conv)
    act = _silu(_layernorm(y_ref[...], lng_ref[...], lnb_ref[...]))
    o_b = jnp.dot(act.astype(BF16), wpw_ref[...], preferred_element_type=F32)
    o_ref[:, d_pool:d_pool + d_conv] = (o_b * z_ref[:, off_gate_b:off_gate_b + d_conv]).astype(BF16)

    for h in range(N_XHEADS):
        c0, c1 = h * xhead_dim, (h + 1) * xhead_dim
        o_c = _attend(z_ref[:, off_q + c0:off_q + c1], k_ref[0, :, c0:c1], v_ref[0, :, c0:c1], xhead_dim)
        o_c = o_c * z_ref[:, off_gate_c + c0:off_gate_c + c1]
        o_ref[:, d_pool + d_conv + c0:d_pool + d_conv + c1] = o_c.astype(BF16)

    @pl.when(t_idx == n_t - 1)
    def _():
        for cb in range(d_pool // LANES):
            pstate_ref[0, :, cb * LANES:(cb + 1) * LANES] = extp_ref[cb, tt + 1:tt + POOL_HALO, :]
        for cb in range(d_conv // LANES):
            cstate_ref[0, :, cb * LANES:(cb + 1) * LANES] = extc_ref[cb, tt + 2:tt + CONV_HALO, :]

    extp_ref[:, 0:POOL_HALO, :] = extp_ref[:, tt:tt + POOL_HALO, :]
    extc_ref[:, 0:CONV_HALO, :] = extc_ref[:, tt:tt + CONV_HALO, :]


def _prompt_mixer(z, k, v, w_pool, pool_scale, w_dw, b_dw, ln_g, ln_b, w_pw, *, batch, seq, tt, d_model):
    d_in = z.shape[1]
    d_pool = w_pool.shape[0] * w_pool.shape[1]
    d_conv = w_pw.shape[0]
    d_xattn = k.shape[2]
    n_mem = k.shape[1]
    n_t = seq // tt
    assert seq % tt == 0 and tt % CONV_CHUNK == 0 and tt % POOL_CHUNK == 0
    kernel = functools.partial(_prompt_mixer_kernel, tt=tt, d_pool=d_pool, d_conv=d_conv, d_xattn=d_xattn)
    const2 = lambda b, t: (0, 0)
    return pl.pallas_call(
        kernel,
        grid=(batch, n_t),
        in_specs=[
            pl.BlockSpec((tt, d_in), lambda b, t: (b * n_t + t, 0)),
            pl.BlockSpec((1, n_mem, d_xattn), lambda b, t: (b, 0, 0)),
            pl.BlockSpec((1, n_mem, d_xattn), lambda b, t: (b, 0, 0)),
            pl.BlockSpec(w_pool.shape, lambda b, t: (0, 0, 0)),
            pl.BlockSpec((1, d_pool), const2),
            pl.BlockSpec(w_dw.shape, const2),
            pl.BlockSpec((1, d_conv), const2),
            pl.BlockSpec((1, d_conv), const2),
            pl.BlockSpec((1, d_conv), const2),
            pl.BlockSpec(w_pw.shape, const2),
        ],
        out_specs=[
            pl.BlockSpec((tt, d_model), lambda b, t: (b * n_t + t, 0)),
            pl.BlockSpec((1, POOL_HALO - 1, d_pool), lambda b, t: (b, 0, 0)),
            pl.BlockSpec((1, CONV_HALO - 2, d_conv), lambda b, t: (b, 0, 0)),
        ],
        out_shape=[
            jax.ShapeDtypeStruct((batch * seq, d_model), BF16),
            jax.ShapeDtypeStruct((batch, POOL_HALO - 1, d_pool), F32),
            jax.ShapeDtypeStruct((batch, CONV_HALO - 2, d_conv), F32),
        ],
        scratch_shapes=[
            pltpu.VMEM((d_pool // LANES, POOL_HALO + tt, LANES), F32),
            pltpu.VMEM((d_conv // LANES, CONV_HALO + tt, LANES), F32),
            pltpu.VMEM((tt, d_pool), F32),
            pltpu.VMEM((tt, d_conv), F32),
        ],
        compiler_params=_compiler_params(("arbitrary", "arbitrary")),
        name="prompt_mixer",
    )(z, k, v, w_pool, pool_scale, w_dw, b_dw, ln_g, ln_b, w_pw)


def _sample_mixer_kernel(z_ref, pst_ref, cst_ref, wpool_ref, pscale_ref, wdw_ref, bdw_ref, lng_ref, lnb_ref, wpw_ref,
                         o_ref, pnew_ref, cnew_ref, *, dec_seq, d_pool, d_conv):
    pg = d_pool // len(POOL_WINDOWS)
    n_pst = pst_ref.shape[0]
    n_cst = cst_ref.shape[0]
    gb = z_ref.shape[1]
    off_gate_a = d_pool
    off_val = 2 * d_pool
    off_glu = off_val + d_conv
    off_gate_b = off_glu + d_conv

    def pool_row(r, c0, c1):
        if r < n_pst:
            return pst_ref[r, :, c0:c1]
        return z_ref[r - n_pst, :, c0:c1]

    for gi, w in enumerate(POOL_WINDOWS):
        c0, c1 = gi * pg, (gi + 1) * pg
        pooled = []
        for t in range(dec_seq):
            s = pool_row(n_pst + t, c0, c1)
            cur = s
            for k in range(1, w):
                s = s + pool_row(n_pst + t - k, c0, c1)
            cnt = float(min(w, PAST_LEN + t + 1))
            pooled.append((s / cnt - cur).astype(BF16))
        mixed = jnp.dot(jnp.concatenate(pooled, axis=0), wpool_ref[gi], preferred_element_type=F32)
        mixed = mixed * pscale_ref[:, c0:c1]
        for t in range(dec_seq):
            gate = z_ref[t, :, off_gate_a + c0:off_gate_a + c1]
            o_ref[t, :, c0:c1] = (mixed[t * gb:(t + 1) * gb] * gate).astype(BF16)
    for r in range(n_pst):
        pnew_ref[r] = pool_row(r + dec_seq, 0, d_pool)

    cw = 2 * LANES
    ys = [[] for _ in range(dec_seq)]
    for cb in range(d_conv // cw):
        c0, c1 = cb * cw, (cb + 1) * cw
        a_new = [z_ref[t, :, off_val + c0:off_val + c1] * z_ref[t, :, off_glu + c0:off_glu + c1]
                 for t in range(dec_seq)]

        def conv_row(r, c0=c0, c1=c1, a_new=a_new):
            if r < n_cst:
                return cst_ref[r, :, c0:c1]
            return a_new[r - n_cst]

        acc = [jnp.broadcast_to(bdw_ref[:, c0:c1], (gb, cw)) for _ in range(dec_seq)]
        for r in range(n_cst + dec_seq):
            x_r = conv_row(r)
            for t in range(dec_seq):
                k = r - t
                if 0 <= k < CONV_WIDTH:
                    acc[t] = acc[t] + x_r * wdw_ref[k:k + 1, c0:c1]
            if r >= dec_seq:
                cnew_ref[r - dec_seq, :, c0:c1] = x_r
        for t in range(dec_seq):
            ys[t].append(acc[t])
    y = jnp.concatenate([jnp.concatenate(ys[t], axis=1) for t in range(dec_seq)], axis=0)
    act = _silu(_layernorm(y, lng_ref[...], lnb_ref[...]))
    o_b = jnp.dot(act.astype(BF16), wpw_ref[...], preferred_element_type=F32)
    for t in range(dec_seq):
        gate = z_ref[t, :, off_gate_b:off_gate_b + d_conv]
        o_ref[t, :, d_pool:d_pool + d_conv] = (o_b[t * gb:(t + 1) * gb] * gate).astype(BF16)


def _sample_mixer(z3, pstate, cstate, w_pool, pool_scale, w_dw, b_dw, ln_g, ln_b, w_pw, *, gb):
    dec_seq, dec_batch, d_in = z3.shape
    d_pool = pstate.shape[2]
    d_conv = cstate.shape[2]
    assert dec_batch % gb == 0
    kernel = functools.partial(_sample_mixer_kernel, dec_seq=dec_seq, d_pool=d_pool, d_conv=d_conv)
    const2 = lambda i: (0, 0)
    seq_block = lambda a: pl.BlockSpec((a.shape[0], gb, a.shape[2]), lambda i: (0, i, 0))
    return pl.pallas_call(
        kernel,
        grid=(dec_batch // gb,),
        in_specs=[
            seq_block(z3),
            seq_block(pstate),
            seq_block(cstate),
            pl.BlockSpec(w_pool.shape, lambda i: (0, 0, 0)),
            pl.BlockSpec((1, d_pool), const2),
            pl.BlockSpec(w_dw.shape, const2),
            pl.BlockSpec((1, d_conv), const2),
            pl.BlockSpec((1, d_conv), const2),
            pl.BlockSpec((1, d_conv), const2),
            pl.BlockSpec(w_pw.shape, const2),
        ],
        out_specs=[
            pl.BlockSpec((dec_seq, gb, d_pool + d_conv), lambda i: (0, i, 0)),
            seq_block(pstate),
            seq_block(cstate),
        ],
        out_shape=[
            jax.ShapeDtypeStruct((dec_seq, dec_batch, d_pool + d_conv), BF16),
            jax.ShapeDtypeStruct(pstate.shape, F32),
            jax.ShapeDtypeStruct(cstate.shape, F32),
        ],
        compiler_params=_compiler_params(("arbitrary",)),
        name="sample_mixer",
    )(z3, pstate, cstate, w_pool, pool_scale, w_dw, b_dw, ln_g, ln_b, w_pw)


def _sample_attn_kernel(q_ref, gate_ref, k_ref, v_ref, o_ref, *, scale):
    gb, n_rows, _ = k_ref.shape
    n_q = q_ref.shape[1]
    half_rows = n_q // 2
    col = lax.broadcasted_iota(jnp.int32, (n_q, n_rows), 1)
    row = lax.broadcasted_iota(jnp.int32, (n_q, n_rows), 0)
    match = (((col >> 2) & 1) == (row >> 4)) & ((col & 3) == ((row >> 2) & 3))
    valid = match[0:half_rows]
    for b in range(gb):
        k = k_ref[b].astype(BF16)
        v = v_ref[b].astype(BF16)
        p = lax.dot_general(q_ref[b].astype(BF16), k, (((1,), (1,)), ((), ())), preferred_element_type=F32)
        p = jnp.where(match, p, 0.0)
        s = p[0:half_rows] + pltpu.roll(p[half_rows:n_q], n_rows - 4, axis=1)
        s = jnp.where(valid, s * scale, -1e30)
        e = jnp.exp(s - jnp.max(s, axis=-1, keepdims=True))
        e = jnp.where(valid, e, 0.0)
        l = jnp.sum(e, axis=-1, keepdims=True)
        e2 = jnp.concatenate([e, pltpu.roll(e, 4, axis=1)], axis=0)
        o = jnp.dot(e2.astype(BF16), v, preferred_element_type=F32) / jnp.concatenate([l, l], axis=0)
        o_ref[b] = o * gate_ref[b]


def _sample_attn(qm, gm, cache_k, cache_v, *, gb, xhead_dim):
    nb, n_q, lanes = qm.shape
    n_rows = cache_k.shape[1]
    assert n_q == 2 * N_XHEADS * 4 and N_XHEADS == 4 and lanes == LANES
    kernel = functools.partial(_sample_attn_kernel, scale=xhead_dim ** -0.5)
    small = pl.BlockSpec((gb, n_q, lanes), lambda i: (i, 0, 0))
    big = pl.BlockSpec((gb, n_rows, lanes), lambda i: (i, 0, 0))
    return pl.pallas_call(
        kernel,
        grid=(nb // gb,),
        in_specs=[small, small, big, big],
        out_specs=small,
        out_shape=jax.ShapeDtypeStruct((nb, n_q, lanes), F32),
        compiler_params=_compiler_params(("arbitrary",)),
        name="sample_attn",
    )(qm, gm, cache_k, cache_v)


def _cache_rows(c):
    nb, n_mem, n_heads, e = c.shape
    assert n_heads == N_XHEADS and e == 2 * LANES
    return c.reshape(nb, n_mem, n_heads, 2, LANES).transpose(0, 1, 3, 2, 4).reshape(nb, n_mem * 2 * n_heads, LANES)


def _query_rows(q):
    t, nb, _ = q.shape
    return q.reshape(t, nb, N_XHEADS, 2, LANES).transpose(1, 3, 2, 0, 4).reshape(nb, 2 * N_XHEADS * t, LANES)


def _query_rows_inv(o, t):
    nb = o.shape[0]
    return o.reshape(nb, 2, N_XHEADS, t, LANES).transpose(3, 0, 2, 1, 4).reshape(t, nb, N_XHEADS * 2 * LANES)


def _out_proj_kernel(o_ref, w_ref, x_ref, g_ref, y_ref, acc_ref, ssq_ref, *, n_j, tn, d_model):
    s = pl.program_id(1)

    @pl.when(s < n_j)
    def _():
        part = x_ref[...] + jnp.dot(o_ref[...], w_ref[...], preferred_element_type=F32)
        acc_ref[s] = part
        row_ssq = jnp.sum(part * part, axis=-1, keepdims=True)

        @pl.when(s == 0)
        def _():
            ssq_ref[...] = row_ssq

        @pl.when(s > 0)
        def _():
            ssq_ref[...] += row_ssq

    @pl.when(s >= n_j)
    def _():
        j = s - n_j
        scale = lax.rsqrt(ssq_ref[...] * (1.0 / d_model) + EPS)
        y_ref[...] = acc_ref[j] * scale * g_ref[...]


def _out_proj(o, w, x, g, *, tile):
    tm, tn = tile
    m, k = o.shape
    n = w.shape[1]
    n_j = n // tn
    assert m % tm == 0 and n % tn == 0
    n_i = m // tm
    kernel = functools.partial(_out_proj_kernel, n_j=n_j, tn=tn, d_model=n)
    row_tile = lambda i, s: jnp.minimum(i + (s >= n_j).astype(jnp.int32), n_i - 1)
    proj_j = lambda s: jnp.where(s < n_j, s, 0)
    return pl.pallas_call(
        kernel,
        grid=(n_i, 2 * n_j),
        in_specs=[
            pl.BlockSpec((tm, k), lambda i, s: (row_tile(i, s), 0)),
            pl.BlockSpec((k, tn), lambda i, s: (0, proj_j(s))),
            pl.BlockSpec((tm, tn), lambda i, s: (row_tile(i, s), proj_j(s))),
            pl.BlockSpec((1, tn), lambda i, s: (0, jnp.maximum(s - n_j, 0))),
        ],
        out_specs=pl.BlockSpec((tm, tn), lambda i, s: (i, jnp.maximum(s - n_j, 0))),
        out_shape=jax.ShapeDtypeStruct((m, n), F32),
        scratch_shapes=[pltpu.VMEM((n_j, tm, tn), F32), pltpu.VMEM((tm, 1), F32)],
        compiler_params=_compiler_params(("arbitrary", "arbitrary")),
        name="out_proj",
    )(o, w, x, g)


def kernel(x_prompt, mem_prompt, x_sample, cache_mem_k, cache_mem_v, state_pool, state_conv, norm_g, mem_norm_g, w_in,
           w_mem_k, w_mem_v, w_pool, pool_scale, w_dw, b_dw, conv_ln_g, conv_ln_b, w_pw, w_out, final_norm_g):
    depth = w_in.shape[0]
    assert depth == 1, "single-layer step"
    batch, seq, d_model = x_prompt.shape
    dec_batch, dec_seq, _ = x_sample.shape
    n_mem = mem_prompt.shape[1]
    d_xattn = w_mem_k.shape[2]
    d_pool = pool_scale.shape[1]
    d_conv = w_pw.shape[1]
    xhead_dim = d_xattn // N_XHEADS
    off_q = 2 * d_pool + 3 * d_conv
    l = 0

    w_out_b = w_out[l].astype(BF16)
    w_pool_b = w_pool[l].astype(BF16)
    w_pw_b = w_pw[l].astype(BF16)
    g_in = norm_g[l][None, :]
    g_mem = mem_norm_g[l][None, :]
    g_fin = final_norm_g[None, :]
    pscale = pool_scale[l][None, :]
    bdw = b_dw[l][None, :]
    lng = conv_ln_g[l][None, :]
    lnb = conv_ln_b[l][None, :]
    mixer_w = (w_pool_b, pscale, w_dw[l], bdw, lng, lnb, w_pw_b)
    z_segments = ((d_pool, ACT_NONE), (d_pool, ACT_SILU), (d_conv, ACT_NONE), (d_conv, ACT_SIGMOID),
                  (d_conv, ACT_SILU), (d_xattn, ACT_NONE), (d_xattn, ACT_SILU))

    xp = x_prompt.reshape(batch * seq, d_model)
    mem = mem_prompt.reshape(batch * n_mem, d_model)
    k_p = _norm_matmul(mem, g_mem, w_mem_k[l], tile=FUSED_PROJ_TILE).reshape(batch, n_mem, d_xattn)
    v_p = _norm_matmul(mem, g_mem, w_mem_v[l], tile=FUSED_PROJ_TILE).reshape(batch, n_mem, d_xattn)
    z_p = _stream_norm_matmul(xp, g_in, w_in[l], tile=PROMPT_PROJ_TILE, n_chunk=PROMPT_PROJ_NORM_CHUNKS,
                              act_ranges=_act_tile_ranges(z_segments, PROMPT_PROJ_TILE[1]))
    o_p, pool_p, conv_p = _prompt_mixer(z_p, k_p.astype(BF16), v_p.astype(BF16), *mixer_w,
                                        batch=batch, seq=seq, tt=PROMPT_MIXER_ROWS, d_model=d_model)
    y_p = _out_proj(o_p, w_out_b, xp, g_fin, tile=OUT_PROJ_TILE).reshape(batch, seq, d_model)

    xs = x_sample.transpose(1, 0, 2).reshape(dec_seq * dec_batch, d_model)
    z_s = _norm_matmul(xs, g_in, w_in[l], tile=FUSED_PROJ_TILE,
                       act_ranges=_act_tile_ranges(z_segments, FUSED_PROJ_TILE[1])).reshape(dec_seq, dec_batch, -1)
    o_ab, pool_s, conv_s = _sample_mixer(z_s, state_pool[l].transpose(1, 0, 2), state_conv[l].transpose(1, 0, 2),
                                         *mixer_w, gb=SAMPLE_MIXER_SEQS)
    o_c = _sample_attn(_query_rows(z_s[:, :, off_q:off_q + d_xattn]), _query_rows(z_s[:, :, off_q + d_xattn:]),
                       _cache_rows(cache_mem_k[l]), _cache_rows(cache_mem_v[l]),
                       gb=SAMPLE_ATTN_SEQS, xhead_dim=xhead_dim)
    o_c = _query_rows_inv(o_c, dec_seq)
    o_s = jnp.concatenate([o_ab, o_c.astype(BF16)], axis=-1).reshape(dec_seq * dec_batch, d_model)
    y_s = _out_proj(o_s, w_out_b, xs, g_fin, tile=SAMPLE_OUT_PROJ_TILE)
    y_s = y_s.reshape(dec_seq, dec_batch, d_model).transpose(1, 0, 2)

    mem_shape = (depth, batch, n_mem, N_XHEADS, xhead_dim)
    return (y_p, y_s, k_p.reshape(mem_shape), v_p.reshape(mem_shape), pool_p[None], conv_p[None],
            pool_s.transpose(1, 0, 2)[None], conv_s.transpose(1, 0, 2)[None])
```

```python
import functools

import jax
import jax.numpy as jnp
from jax import lax
from jax.experimental import pallas as pl
from jax.experimental.pallas import tpu as pltpu

EPS = 1e-6
POOL_WINDOWS = (2, 4, 8, 16)
CONV_WIDTH = 31
N_XHEADS = 4
PAST_LEN = 16384

SUBLANES = 8
LANES = 128
VMEM_LIMIT_BYTES = 56 * 1024 * 1024

PROMPT_PROJ_TILE = (1024, 512)
PROMPT_PROJ_NORM_CHUNKS = 16
W_OUT_CAST_ROWS = 64
MEM_PROJ_TILE = (512, 256)
OUT_PROJ_TILE = (1024, 512)
SAMPLE_OUT_PROJ_TILE = (512, 512)
PROMPT_MIXER_ROWS = 256
POOL_CHUNK = 32
CONV_CHUNK = 64
SAMPLE_MIXER_SEQS = 16
SAMPLE_ATTN_SEQS = 8

BF16 = jnp.bfloat16
F32 = jnp.float32

ACT_NONE, ACT_SILU, ACT_SIGMOID = 0, 1, 2


def _sigmoid(x):
    return 0.5 * jnp.tanh(0.5 * x) + 0.5


def _silu(x):
    return x * _sigmoid(x)


def _activate(r, act):
    if act == ACT_SILU:
        return _silu(r)
    if act == ACT_SIGMOID:
        return _sigmoid(r)
    return r


def _rms_scale(x):
    return lax.rsqrt(jnp.mean(x * x, axis=-1, keepdims=True) + EPS)


def _compiler_params(semantics):
    return pltpu.CompilerParams(dimension_semantics=semantics, vmem_limit_bytes=VMEM_LIMIT_BYTES)


def _act_tile_ranges(segments, tn):
    ranges, off = [], 0
    for width, act in segments:
        assert off % tn == 0 and width % tn == 0, "segments must be whole column tiles"
        if act != ACT_NONE:
            ranges.append((off // tn, (off + width) // tn, act))
        off += width
    return tuple(ranges)


def _tile_act(j, act_ranges):
    act = jnp.int32(ACT_NONE)
    for lo, hi, a in act_ranges:
        act = jnp.where((j >= lo) & (j < hi), a, act)
    return act


def _acts_used(act_ranges):
    return sorted({ACT_NONE} | {a for _, _, a in act_ranges})


def _norm_matmul_pair_kernel(x_ref, g_ref, wa_ref, wb_ref, oa_ref, ob_ref, h_ref):
    @pl.when(pl.program_id(1) == 0)
    def _():
        x = x_ref[...]
        h_ref[...] = (x * _rms_scale(x) * g_ref[...]).astype(BF16)

    oa_ref[...] = jnp.dot(h_ref[...], wa_ref[...].astype(BF16), preferred_element_type=F32)
    ob_ref[...] = jnp.dot(h_ref[...], wb_ref[...].astype(BF16), preferred_element_type=F32)


def _stream_norm_matmul_kernel(x_ref, xs_ref, g_ref, w_ref, cast_src_ref, o_ref, os_ref, cast_dst_ref, h_ref, hs_ref,
                               *, n_j, n_chunk, n_chunk_s, chunk, act_ranges):
    s = pl.program_id(0)
    n_pro = n_chunk_s + n_chunk
    step = jnp.maximum(s - n_pro, 0)
    i = step // n_j
    j = step % n_j

    def normalised(src_ref):
        x = src_ref[...]
        return (x * _rms_scale(x) * g_ref[...]).astype(BF16)

    def chunk_rows(c):
        return pl.ds(pl.multiple_of(c * chunk, chunk), chunk)

    @pl.when(s < n_chunk_s)
    def _():
        hs_ref[chunk_rows(s), :] = normalised(xs_ref)

    @pl.when((s >= n_chunk_s) & (s < n_pro))
    def _():
        h_ref[0, chunk_rows(s - n_chunk_s), :] = normalised(x_ref)

    tile_act = _tile_act(j, act_ranges)
    for act in _acts_used(act_ranges):

        @pl.when((s >= n_pro) & (tile_act == act))
        def _(act=act):
            h_ref[(i + 1) % 2, chunk_rows(jnp.clip(j - 1, 0, n_chunk - 1)), :] = normalised(x_ref)
            cast_dst_ref[...] = cast_src_ref[...].astype(BF16)
            w = w_ref[...].astype(BF16)
            o_ref[...] = _activate(jnp.dot(h_ref[i % 2], w, preferred_element_type=F32), act)

            @pl.when(i == 0)
            def _():
                os_ref[...] = _activate(jnp.dot(hs_ref[...], w, preferred_element_type=F32), act)


def _stream_norm_matmul(x, xs, g, w, cast_src, *, tile, n_chunk, cast_rows, act_ranges=()):
    tm, tn = tile
    m, k = x.shape
    ms = xs.shape[0]
    n = w.shape[1]
    n_i, n_j = m // tm, n // tn
    chunk = tm // n_chunk
    n_chunk_s = ms // chunk
    n_pro = n_chunk_s + n_chunk
    n_cast = cast_src.shape[0] // cast_rows
    assert m % tm == 0 and n % tn == 0 and tm % n_chunk == 0 and n_chunk < n_j
    assert ms % chunk == 0 and ms <= tm and cast_src.shape[0] % cast_rows == 0 and n_cast <= n_i * n_j

    def steps(s):
        step = jnp.maximum(s - n_pro, 0)
        return step // n_j, step % n_j

    def x_index(s):
        i, j = steps(s)
        nxt = jnp.minimum((i + 1) * n_chunk + jnp.clip(j - 1, 0, n_chunk - 1), n_i * n_chunk - 1)
        return jnp.where(s < n_pro, jnp.maximum(s - n_chunk_s, 0), nxt), 0

    def os_index(s):
        i, j = steps(s)
        return 0, jnp.where(i == 0, j, n_j - 1)

    cast_index = lambda s: (jnp.minimum(jnp.maximum(s - n_pro, 0), n_cast - 1), 0)
    kernel = functools.partial(_stream_norm_matmul_kernel, n_j=n_j, n_chunk=n_chunk, n_chunk_s=n_chunk_s,
                               chunk=chunk, act_ranges=act_ranges)
    return pl.pallas_call(
        kernel,
        grid=(n_pro + n_i * n_j,),
        in_specs=[
            pl.BlockSpec((chunk, k), x_index),
            pl.BlockSpec((chunk, k), lambda s: (jnp.minimum(s, n_chunk_s - 1), 0)),
            pl.BlockSpec((1, k), lambda s: (0, 0)),
            pl.BlockSpec((k, tn), lambda s: (0, steps(s)[1])),
            pl.BlockSpec((cast_rows, cast_src.shape[1]), cast_index),
        ],
        out_specs=[
            pl.BlockSpec((tm, tn), lambda s: steps(s)),
            pl.BlockSpec((ms, tn), os_index),
            pl.BlockSpec((cast_rows, cast_src.shape[1]), cast_index),
        ],
        out_shape=[
            jax.ShapeDtypeStruct((m, n), F32),
            jax.ShapeDtypeStruct((ms, n), F32),
            jax.ShapeDtypeStruct(cast_src.shape, BF16),
        ],
        scratch_shapes=[pltpu.VMEM((2, tm, k), BF16), pltpu.VMEM((ms, k), BF16)],
        compiler_params=_compiler_params(("arbitrary",)),
        name="stream_norm_matmul",
    )(x, xs, g, w, cast_src)


def _norm_matmul_pair(x, g, w_a, w_b, *, tile):
    tm, tn = tile
    m, k = x.shape
    n = w_a.shape[1]
    assert m % tm == 0 and n % tn == 0 and w_b.shape == w_a.shape
    w_spec = pl.BlockSpec((k, tn), lambda i, j: (0, j))
    o_spec = pl.BlockSpec((tm, tn), lambda i, j: (i, j))
    return pl.pallas_call(
        _norm_matmul_pair_kernel,
        grid=(m // tm, n // tn),
        in_specs=[pl.BlockSpec((tm, k), lambda i, j: (i, 0)), pl.BlockSpec((1, k), lambda i, j: (0, 0)), w_spec, w_spec],
        out_specs=[o_spec, o_spec],
        out_shape=[jax.ShapeDtypeStruct((m, n), F32)] * 2,
        scratch_shapes=[pltpu.VMEM((tm, k), BF16)],
        compiler_params=_compiler_params(("arbitrary", "arbitrary")),
        name="norm_matmul_pair",
    )(x, g, w_a, w_b)


POOL_HALO = 16
CONV_HALO = 32


def _pool_tile(extp_ref, pooled_ref, tt, d_pool, t_idx):
    pg = d_pool // len(POOL_WINDOWS)
    row = lax.broadcasted_iota(jnp.int32, (POOL_CHUNK, LANES), 0)

    def body(i, carry):
        base = pl.multiple_of(i * POOL_CHUNK, POOL_CHUNK)
        n_prev = t_idx * tt + base + row + 1
        for gi, w in enumerate(POOL_WINDOWS):
            inv_cnt = 1.0 / jnp.minimum(w, n_prev).astype(F32)
            for cb in range(gi * pg // LANES, (gi + 1) * pg // LANES):
                cur = extp_ref[cb, pl.ds(base + POOL_HALO, POOL_CHUNK), :]
                s = cur
                for k in range(1, w):
                    s = s + extp_ref[cb, pl.ds(base + POOL_HALO - k, POOL_CHUNK), :]
                pooled_ref[pl.ds(base, POOL_CHUNK), cb * LANES:(cb + 1) * LANES] = s * inv_cnt - cur
        return carry

    lax.fori_loop(0, tt // POOL_CHUNK, body, 0)


def _dwconv_tile(extc_ref, wdw_ref, bdw_ref, y_ref, tt, d_conv):
    first = CONV_HALO - (CONV_WIDTH - 1)
    for cb in range(d_conv // LANES):
        c0, c1 = cb * LANES, (cb + 1) * LANES
        for base in range(0, tt, CONV_CHUNK):
            acc = jnp.broadcast_to(bdw_ref[:, c0:c1], (CONV_CHUNK, LANES))
            for k in range(CONV_WIDTH):
                acc = acc + extc_ref[cb, base + first + k:base + first + k + CONV_CHUNK, :] * wdw_ref[k:k + 1, c0:c1]
            y_ref[base:base + CONV_CHUNK, c0:c1] = acc


def _layernorm(y, g, b):
    mu = jnp.mean(y, axis=-1, keepdims=True)
    d = y - mu
    var = jnp.mean(d * d, axis=-1, keepdims=True)
    return d * lax.rsqrt(var + EPS) * g + b


def _attend(q, k, v, xhead_dim):
    s = lax.dot_general(q.astype(BF16), k, (((1,), (1,)), ((), ())), preferred_element_type=F32)
    s = s * (xhead_dim ** -0.5)
    e = jnp.exp(s - jnp.max(s, axis=-1, keepdims=True))
    l = jnp.sum(e, axis=-1, keepdims=True)
    return jnp.dot(e.astype(BF16), v, preferred_element_type=F32) / l


def _prompt_mixer_kernel(z_ref, k_ref, v_ref, wpool_ref, pscale_ref, wdw_ref, bdw_ref, lng_ref, lnb_ref, wpw_ref,
                         o_ref, pstate_ref, cstate_ref, extp_ref, extc_ref, pooled_ref, y_ref,
                         *, tt, d_pool, d_conv, d_xattn):
    t_idx = pl.program_id(1)
    n_t = pl.num_programs(1)
    pg = d_pool // len(POOL_WINDOWS)
    xhead_dim = d_xattn // N_XHEADS
    off_gate_a = d_pool
    off_val = 2 * d_pool
    off_glu = off_val + d_conv
    off_gate_b = off_glu + d_conv
    off_q = off_gate_b + d_conv
    off_gate_c = off_q + d_xattn

    @pl.when(t_idx == 0)
    def _():
        extp_ref[:, 0:POOL_HALO, :] = jnp.zeros((d_pool // LANES, POOL_HALO, LANES), F32)
        extc_ref[:, 0:CONV_HALO, :] = jnp.zeros((d_conv // LANES, CONV_HALO, LANES), F32)

    for cb in range(d_pool // LANES):
        extp_ref[cb, POOL_HALO:POOL_HALO + tt, :] = z_ref[:, cb * LANES:(cb + 1) * LANES]
    _pool_tile(extp_ref, pooled_ref, tt, d_pool, t_idx)
    for gi in range(len(POOL_WINDOWS)):
        c0, c1 = gi * pg, (gi + 1) * pg
        mixed = jnp.dot(pooled_ref[:, c0:c1].astype(BF16), wpool_ref[gi], preferred_element_type=F32)
        mixed = mixed * pscale_ref[:, c0:c1]
        o_ref[:, c0:c1] = (mixed * z_ref[:, off_gate_a + c0:off_gate_a + c1]).astype(BF16)

    for cb in range(d_conv // LANES):
        c0, c1 = cb * LANES, (cb + 1) * LANES
        extc_ref[cb, CONV_HALO:CONV_HALO + tt, :] = z_ref[:, off_val + c0:off_val + c1] * z_ref[:, off_glu + c0:off_glu + c1]
    _dwconv_tile(extc_ref, wdw_ref, bdw_ref, y_ref, tt, d_conv)
    act = _silu(_layernorm(y_ref[...], lng_ref[...], lnb_ref[...]))
    o_b = jnp.dot(act.astype(BF16), wpw_ref[...], preferred_element_type=F32)
    o_ref[:, d_pool:d_pool + d_conv] = (o_b * z_ref[:, off_gate_b:off_gate_b + d_conv]).astype(BF16)

    for h in range(N_XHEADS):
        c0, c1 = h * xhead_dim, (h + 1) * xhead_dim
        o_c = _attend(z_ref[:, off_q + c0:off_q + c1], k_ref[0, :, c0:c1], v_ref[0, :, c0:c1], xhead_dim)
        o_c = o_c * z_ref[:, off_gate_c + c0:off_gate_c + c1]
        o_ref[:, d_pool + d_conv + c0:d_pool + d_conv + c1] = o_c.astype(BF16)

    @pl.when(t_idx == n_t - 1)
    def _():
        for cb in range(d_pool // LANES):
            pstate_ref[0, :, cb * LANES:(cb + 1) * LANES] = extp_ref[cb, tt + 1:tt + POOL_HALO, :]
        for cb in range(d_conv // LANES):
            cstate_ref[0, :, cb * LANES:(cb + 1) * LANES] = extc_ref[cb, tt + 2:tt + CONV_HALO, :]

    extp_ref[:, 0:POOL_HALO, :] = extp_ref[:, tt:tt + POOL_HALO, :]
    extc_ref[:, 0:CONV_HALO, :] = extc_ref[:, tt:tt + CONV_HALO, :]


def _prompt_mixer(z, k, v, w_pool, pool_scale, w_dw, b_dw, ln_g, ln_b, w_pw, *, batch, seq, tt, d_model):
    d_in = z.shape[1]
    d_pool = w_pool.shape[0] * w_pool.shape[1]
    d_conv = w_pw.shape[0]
    d_xattn = k.shape[2]
    n_mem = k.shape[1]
    n_t = seq // tt
    assert seq % tt == 0 and tt % CONV_CHUNK == 0 and tt % POOL_CHUNK == 0
    kernel = functools.partial(_prompt_mixer_kernel, tt=tt, d_pool=d_pool, d_conv=d_conv, d_xattn=d_xattn)
    const2 = lambda b, t: (0, 0)
    return pl.pallas_call(
        kernel,
        grid=(batch, n_t),
        in_specs=[
            pl.BlockSpec((tt, d_in), lambda b, t: (b * n_t + t, 0)),
            pl.BlockSpec((1, n_mem, d_xattn), lambda b, t: (b, 0, 0)),
            pl.BlockSpec((1, n_mem, d_xattn), lambda b, t: (b, 0, 0)),
            pl.BlockSpec(w_pool.shape, lambda b, t: (0, 0, 0)),
            pl.BlockSpec((1, d_pool), const2),
            pl.BlockSpec(w_dw.shape, const2),
            pl.BlockSpec((1, d_conv), const2),
            pl.BlockSpec((1, d_conv), const2),
            pl.BlockSpec((1, d_conv), const2),
            pl.BlockSpec(w_pw.shape, const2),
        ],
        out_specs=[
            pl.BlockSpec((tt, d_model), lambda b, t: (b * n_t + t, 0)),
            pl.BlockSpec((1, POOL_HALO - 1, d_pool), lambda b, t: (b, 0, 0)),
            pl.BlockSpec((1, CONV_HALO - 2, d_conv), lambda b, t: (b, 0, 0)),
        ],
        out_shape=[
            jax.ShapeDtypeStruct((batch * seq, d_model), BF16),
            jax.ShapeDtypeStruct((batch, POOL_HALO - 1, d_pool), F32),
            jax.ShapeDtypeStruct((batch, CONV_HALO - 2, d_conv), F32),
        ],
        scratch_shapes=[
            pltpu.VMEM((d_pool // LANES, POOL_HALO + tt, LANES), F32),
            pltpu.VMEM((d_conv // LANES, CONV_HALO + tt, LANES), F32),
            pltpu.VMEM((tt, d_pool), F32),
            pltpu.VMEM((tt, d_conv), F32),
        ],
        compiler_params=_compiler_params(("arbitrary", "arbitrary")),
        name="prompt_mixer",
    )(z, k, v, w_pool, pool_scale, w_dw, b_dw, ln_g, ln_b, w_pw)


def _sample_mixer_kernel(z_ref, pst_ref, cst_ref, wpool_ref, pscale_ref, wdw_ref, bdw_ref, lng_ref, lnb_ref, wpw_ref,
                         o_ref, pnew_ref, cnew_ref, *, dec_seq, d_pool, d_conv):
    pg = d_pool // len(POOL_WINDOWS)
    n_pst = pst_ref.shape[0]
    n_cst = cst_ref.shape[0]
    gb = z_ref.shape[1]
    off_gate_a = d_pool
    off_val = 2 * d_pool
    off_glu = off_val + d_conv
    off_gate_b = off_glu + d_conv

    def pool_row(r, c0, c1):
        if r < n_pst:
            return pst_ref[r, :, c0:c1]
        return z_ref[r - n_pst, :, c0:c1]

    for gi, w in enumerate(POOL_WINDOWS):
        c0, c1 = gi * pg, (gi + 1) * pg
        pooled = []
        for t in range(dec_seq):
            s = pool_row(n_pst + t, c0, c1)
            cur = s
            for k in range(1, w):
                s = s + pool_row(n_pst + t - k, c0, c1)
            cnt = float(min(w, PAST_LEN + t + 1))
            pooled.append((s / cnt - cur).astype(BF16))
        mixed = jnp.dot(jnp.concatenate(pooled, axis=0), wpool_ref[gi], preferred_element_type=F32)
        mixed = mixed * pscale_ref[:, c0:c1]
        for t in range(dec_seq):
            gate = z_ref[t, :, off_gate_a + c0:off_gate_a + c1]
            o_ref[t, :, c0:c1] = (mixed[t * gb:(t + 1) * gb] * gate).astype(BF16)
    for r in range(n_pst):
        pnew_ref[r] = pool_row(r + dec_seq, 0, d_pool)

    cw = 2 * LANES
    ys = [[] for _ in range(dec_seq)]
    for cb in range(d_conv // cw):
        c0, c1 = cb * cw, (cb + 1) * cw
        a_new = [z_ref[t, :, off_val + c0:off_val + c1] * z_ref[t, :, off_glu + c0:off_glu + c1]
                 for t in range(dec_seq)]

        def conv_row(r, c0=c0, c1=c1, a_new=a_new):
            if r < n_cst:
                return cst_ref[r, :, c0:c1]
            return a_new[r - n_cst]

        acc = [jnp.broadcast_to(bdw_ref[:, c0:c1], (gb, cw)) for _ in range(dec_seq)]
        for r in range(n_cst + dec_seq):
            x_r = conv_row(r)
            for t in range(dec_seq):
                k = r - t
                if 0 <= k < CONV_WIDTH:
                    acc[t] = acc[t] + x_r * wdw_ref[k:k + 1, c0:c1]
            if r >= dec_seq:
                cnew_ref[r - dec_seq, :, c0:c1] = x_r
        for t in range(dec_seq):
            ys[t].append(acc[t])
    y = jnp.concatenate([jnp.concatenate(ys[t], axis=1) for t in range(dec_seq)], axis=0)
    act = _silu(_layernorm(y, lng_ref[...], lnb_ref[...]))
    o_b = jnp.dot(act.astype(BF16), wpw_ref[...], preferred_element_type=F32)
    for t in range(dec_seq):
        gate = z_ref[t, :, off_gate_b:off_gate_b + d_conv]
        o_ref[t, :, d_pool:d_pool + d_conv] = (o_b[t * gb:(t + 1) * gb] * gate).astype(BF16)


def _sample_mixer(z3, pstate, cstate, w_pool, pool_scale, w_dw, b_dw, ln_g, ln_b, w_pw, *, gb):
    dec_seq, dec_batch, d_in = z3.shape
    d_pool = pstate.shape[2]
    d_conv = cstate.shape[2]
    assert dec_batch % gb == 0
    kernel = functools.partial(_sample_mixer_kernel, dec_seq=dec_seq, d_pool=d_pool, d_conv=d_conv)
    const2 = lambda i: (0, 0)
    seq_block = lambda a: pl.BlockSpec((a.shape[0], gb, a.shape[2]), lambda i: (0, i, 0))
    return pl.pallas_call(
        kernel,
        grid=(dec_batch // gb,),
        in_specs=[
            seq_block(z3),
            seq_block(pstate),
            seq_block(cstate),
            pl.BlockSpec(w_pool.shape, lambda i: (0, 0, 0)),
            pl.BlockSpec((1, d_pool), const2),
            pl.BlockSpec(w_dw.shape, const2),
            pl.BlockSpec((1, d_conv), const2),
            pl.BlockSpec((1, d_conv), const2),
            pl.BlockSpec((1, d_conv), const2),
            pl.BlockSpec(w_pw.shape, const2),
        ],
        out_specs=[
            pl.BlockSpec((dec_seq, gb, d_pool + d_conv), lambda i: (0, i, 0)),
            seq_block(pstate),
            seq_block(cstate),
        ],
        out_shape=[
            jax.ShapeDtypeStruct((dec_seq, dec_batch, d_pool + d_conv), BF16),
            jax.ShapeDtypeStruct(pstate.shape, F32),
            jax.ShapeDtypeStruct(cstate.shape, F32),
        ],
        compiler_params=_compiler_params(("arbitrary",)),
        name="sample_mixer",
    )(z3, pstate, cstate, w_pool, pool_scale, w_dw, b_dw, ln_g, ln_b, w_pw)


def _sample_attn_kernel(q_ref, gate_ref, k_ref, v_ref, o_ref, *, scale):
    gb, n_rows, _ = k_ref.shape
    n_q = q_ref.shape[1]
    half_rows = n_q // 2
    col = lax.broadcasted_iota(jnp.int32, (n_q, n_rows), 1)
    row = lax.broadcasted_iota(jnp.int32, (n_q, n_rows), 0)
    match = (((col >> 2) & 1) == (row >> 4)) & ((col & 3) == ((row >> 2) & 3))
    valid = match[0:half_rows]
    for b in range(gb):
        k = k_ref[b].astype(BF16)
        v = v_ref[b].astype(BF16)
        p = lax.dot_general(q_ref[b].astype(BF16), k, (((1,), (1,)), ((), ())), preferred_element_type=F32)
        p = jnp.where(match, p, 0.0)
        s = p[0:half_rows] + pltpu.roll(p[half_rows:n_q], n_rows - 4, axis=1)
        s = jnp.where(valid, s * scale, -1e30)
        e = jnp.exp(s - jnp.max(s, axis=-1, keepdims=True))
        e = jnp.where(valid, e, 0.0)
        l = jnp.sum(e, axis=-1, keepdims=True)
        e2 = jnp.concatenate([e, pltpu.roll(e, 4, axis=1)], axis=0)
        o = jnp.dot(e2.astype(BF16), v, preferred_element_type=F32) / jnp.concatenate([l, l], axis=0)
        o_ref[b] = o * gate_ref[b]


def _sample_attn(qm, gm, cache_k, cache_v, *, gb, xhead_dim):
    nb, n_q, lanes = qm.shape
    n_rows = cache_k.shape[1]
    assert n_q == 2 * N_XHEADS * 4 and N_XHEADS == 4 and lanes == LANES
    kernel = functools.partial(_sample_attn_kernel, scale=xhead_dim ** -0.5)
    small = pl.BlockSpec((gb, n_q, lanes), lambda i: (i, 0, 0))
    big = pl.BlockSpec((gb, n_rows, lanes), lambda i: (i, 0, 0))
    return pl.pallas_call(
        kernel,
        grid=(nb // gb,),
        in_specs=[small, small, big, big],
        out_specs=small,
        out_shape=jax.ShapeDtypeStruct((nb, n_q, lanes), F32),
        compiler_params=_compiler_params(("arbitrary",)),
        name="sample_attn",
    )(qm, gm, cache_k, cache_v)


def _cache_rows(c):
    nb, n_mem, n_heads, e = c.shape
    assert n_heads == N_XHEADS and e == 2 * LANES
    return c.reshape(nb, n_mem, n_heads, 2, LANES).transpose(0, 1, 3, 2, 4).reshape(nb, n_mem * 2 * n_heads, LANES)


def _query_rows(q):
    t, nb, _ = q.shape
    return q.reshape(t, nb, N_XHEADS, 2, LANES).transpose(1, 3, 2, 0, 4).reshape(nb, 2 * N_XHEADS * t, LANES)


def _query_rows_inv(o, t):
    nb = o.shape[0]
    return o.reshape(nb, 2, N_XHEADS, t, LANES).transpose(3, 0, 2, 1, 4).reshape(t, nb, N_XHEADS * 2 * LANES)


def _out_proj_kernel(o_ref, w_ref, x_ref, g_ref, y_ref, acc_ref, ssq_ref, *, n_j, tn, d_model):
    s = pl.program_id(1)

    @pl.when(s < n_j)
    def _():
        part = x_ref[...] + jnp.dot(o_ref[...], w_ref[...], preferred_element_type=F32)
        acc_ref[s] = part
        row_ssq = jnp.sum(part * part, axis=-1, keepdims=True)

        @pl.when(s == 0)
        def _():
            ssq_ref[...] = row_ssq

        @pl.when(s > 0)
        def _():
            ssq_ref[...] += row_ssq

    @pl.when(s >= n_j)
    def _():
        j = s - n_j
        scale = lax.rsqrt(ssq_ref[...] * (1.0 / d_model) + EPS)
        y_ref[...] = acc_ref[j] * scale * g_ref[...]


def _out_proj(o, w, x, g, *, tile):
    tm, tn = tile
    m, k = o.shape
    n = w.shape[1]
    n_j = n // tn
    assert m % tm == 0 and n % tn == 0
    n_i = m // tm
    kernel = functools.partial(_out_proj_kernel, n_j=n_j, tn=tn, d_model=n)
    row_tile = lambda i, s: jnp.minimum(i + (s >= n_j).astype(jnp.int32), n_i - 1)
    proj_j = lambda s: jnp.where(s < n_j, s, 0)
    return pl.pallas_call(
        kernel,
        grid=(n_i, 2 * n_j),
        in_specs=[
            pl.BlockSpec((tm, k), lambda i, s: (row_tile(i, s), 0)),
            pl.BlockSpec((k, tn), lambda i, s: (0, proj_j(s))),
            pl.BlockSpec((tm, tn), lambda i, s: (row_tile(i, s), proj_j(s))),
            pl.BlockSpec((1, tn), lambda i, s: (0, jnp.maximum(s - n_j, 0))),
        ],
        out_specs=pl.BlockSpec((tm, tn), lambda i, s: (i, jnp.maximum(s - n_j, 0))),
        out_shape=jax.ShapeDtypeStruct((m, n), F32),
        scratch_shapes=[pltpu.VMEM((n_j, tm, tn), F32), pltpu.VMEM((tm, 1), F32)],
        compiler_params=_compiler_params(("arbitrary", "arbitrary")),
        name="out_proj",
    )(o, w, x, g)


def kernel(x_prompt, mem_prompt, x_sample, cache_mem_k, cache_mem_v, state_pool, state_conv, norm_g, mem_norm_g, w_in,
           w_mem_k, w_mem_v, w_pool, pool_scale, w_dw, b_dw, conv_ln_g, conv_ln_b, w_pw, w_out, final_norm_g):
    depth = w_in.shape[0]
    assert depth == 1, "single-layer step"
    batch, seq, d_model = x_prompt.shape
    dec_batch, dec_seq, _ = x_sample.shape
    n_mem = mem_prompt.shape[1]
    d_xattn = w_mem_k.shape[2]
    d_pool = pool_scale.shape[1]
    d_conv = w_pw.shape[1]
    xhead_dim = d_xattn // N_XHEADS
    off_q = 2 * d_pool + 3 * d_conv
    l = 0

    w_pool_b = w_pool[l].astype(BF16)
    w_pw_b = w_pw[l].astype(BF16)
    g_in = norm_g[l][None, :]
    g_mem = mem_norm_g[l][None, :]
    g_fin = final_norm_g[None, :]
    pscale = pool_scale[l][None, :]
    bdw = b_dw[l][None, :]
    lng = conv_ln_g[l][None, :]
    lnb = conv_ln_b[l][None, :]
    mixer_w = (w_pool_b, pscale, w_dw[l], bdw, lng, lnb, w_pw_b)
    z_segments = ((d_pool, ACT_NONE), (d_pool, ACT_SILU), (d_conv, ACT_NONE), (d_conv, ACT_SIGMOID),
                  (d_conv, ACT_SILU), (d_xattn, ACT_NONE), (d_xattn, ACT_SILU))

    xp = x_prompt.reshape(batch * seq, d_model)
    mem = mem_prompt.reshape(batch * n_mem, d_model)
    k_p, v_p = _norm_matmul_pair(mem, g_mem, w_mem_k[l], w_mem_v[l], tile=MEM_PROJ_TILE)
    k_p = k_p.reshape(batch, n_mem, d_xattn)
    v_p = v_p.reshape(batch, n_mem, d_xattn)
    xs = x_sample.transpose(1, 0, 2).reshape(dec_seq * dec_batch, d_model)
    z_p, z_s, w_out_b = _stream_norm_matmul(
        xp, xs, g_in, w_in[l], w_out[l], tile=PROMPT_PROJ_TILE, n_chunk=PROMPT_PROJ_NORM_CHUNKS,
        cast_rows=W_OUT_CAST_ROWS, act_ranges=_act_tile_ranges(z_segments, PROMPT_PROJ_TILE[1]))
    o_p, pool_p, conv_p = _prompt_mixer(z_p, k_p.astype(BF16), v_p.astype(BF16), *mixer_w,
                                        batch=batch, seq=seq, tt=PROMPT_MIXER_ROWS, d_model=d_model)
    y_p = _out_proj(o_p, w_out_b, xp, g_fin, tile=OUT_PROJ_TILE).reshape(batch, seq, d_model)

    z_s = z_s.reshape(dec_seq, dec_batch, -1)
    o_ab, pool_s, conv_s = _sample_mixer(z_s, state_pool[l].transpose(1, 0, 2), state_conv[l].transpose(1, 0, 2),
                                         *mixer_w, gb=SAMPLE_MIXER_SEQS)
    o_c = _sample_attn(_query_rows(z_s[:, :, off_q:off_q + d_xattn]), _query_rows(z_s[:, :, off_q + d_xattn:]),
                       _cache_rows(cache_mem_k[l]), _cache_rows(cache_mem_v[l]),
                       gb=SAMPLE_ATTN_SEQS, xhead_dim=xhead_dim)
    o_c = _query_rows_inv(o_c, dec_seq)
    o_s = jnp.concatenate([o_ab, o_c.astype(BF16)], axis=-1).reshape(dec_seq * dec_batch, d_model)
    y_s = _out_proj(o_s, w_out_b, xs, g_fin, tile=SAMPLE_OUT_PROJ_TILE)
    y_s = y_s.reshape(dec_seq, dec_batch, d_model).transpose(1, 0, 2)

    mem_shape = (depth, batch, n_mem, N_XHEADS, xhead_dim)
    return (y_p, y_s, k_p.reshape(mem_shape), v_p.reshape(mem_shape), pool_p[None], conv_p[None],
            pool_s.transpose(1, 0, 2)[None], conv_s.transpose(1, 0, 2)[None])
```

```python
import functools

import jax
import jax.numpy as jnp
from jax import lax
from jax.experimental import pallas as pl
from jax.experimental.pallas import tpu as pltpu

EPS = 1e-6
POOL_WINDOWS = (2, 4, 8, 16)
CONV_WIDTH = 31
N_XHEADS = 4
PAST_LEN = 16384

SUBLANES = 8
LANES = 128
VMEM_LIMIT_BYTES = 56 * 1024 * 1024

PROMPT_PROJ_TILE = (1024, 512)
PROMPT_PROJ_NORM_CHUNKS = 16
W_OUT_CAST_ROWS = 64
MEM_PROJ_TILE = (512, 256)
OUT_PROJ_TILE = (1024, 512)
SAMPLE_OUT_PROJ_TILE = (512, 512)
PROMPT_MIXER_ROWS = 256
POOL_CHUNK = 32
CONV_CHUNK = 64
SAMPLE_MIXER_SEQS = 16
SAMPLE_ATTN_SEQS = 8

BF16 = jnp.bfloat16
F32 = jnp.float32

ACT_NONE, ACT_SILU, ACT_SIGMOID = 0, 1, 2


def _sigmoid(x):
    return 0.5 * jnp.tanh(0.5 * x) + 0.5


def _silu(x):
    return x * _sigmoid(x)


def _activate(r, act):
    if act == ACT_SILU:
        return _silu(r)
    if act == ACT_SIGMOID:
        return _sigmoid(r)
    return r


def _rms_scale(x):
    return lax.rsqrt(jnp.mean(x * x, axis=-1, keepdims=True) + EPS)


def _compiler_params(semantics):
    return pltpu.CompilerParams(dimension_semantics=semantics, vmem_limit_bytes=VMEM_LIMIT_BYTES)


def _act_tile_ranges(segments, tn):
    ranges, off = [], 0
    for width, act in segments:
        assert off % tn == 0 and width % tn == 0, "segments must be whole column tiles"
        if act != ACT_NONE:
            ranges.append((off // tn, (off + width) // tn, act))
        off += width
    return tuple(ranges)


def _tile_act(j, act_ranges):
    act = jnp.int32(ACT_NONE)
    for lo, hi, a in act_ranges:
        act = jnp.where((j >= lo) & (j < hi), a, act)
    return act


def _acts_used(act_ranges):
    return sorted({ACT_NONE} | {a for _, _, a in act_ranges})


def _norm_matmul_pair_kernel(x_ref, g_ref, wa_ref, wb_ref, oa_ref, ob_ref, h_ref):
    @pl.when(pl.program_id(1) == 0)
    def _():
        x = x_ref[...]
        h_ref[...] = (x * _rms_scale(x) * g_ref[...]).astype(BF16)

    oa_ref[...] = jnp.dot(h_ref[...], wa_ref[...].astype(BF16), preferred_element_type=F32)
    ob_ref[...] = jnp.dot(h_ref[...], wb_ref[...].astype(BF16), preferred_element_type=F32)


def _stream_norm_matmul_kernel(x_ref, xs_ref, g_ref, w_ref, cast_src_ref, o_ref, os_ref, cast_dst_ref, h_ref, hs_ref,
                               *, n_j, n_chunk, n_chunk_s, chunk, act_ranges):
    s = pl.program_id(0)
    n_pro = n_chunk_s + n_chunk
    step = jnp.maximum(s - n_pro, 0)
    i = step // n_j
    j = step % n_j

    def normalised(src_ref):
        x = src_ref[...]
        return (x * _rms_scale(x) * g_ref[...]).astype(BF16)

    def chunk_rows(c):
        return pl.ds(pl.multiple_of(c * chunk, chunk), chunk)

    @pl.when(s < n_chunk_s)
    def _():
        hs_ref[chunk_rows(s), :] = normalised(xs_ref)

    @pl.when((s >= n_chunk_s) & (s < n_pro))
    def _():
        h_ref[0, chunk_rows(s - n_chunk_s), :] = normalised(x_ref)

    tile_act = _tile_act(j, act_ranges)
    for act in _acts_used(act_ranges):

        @pl.when((s >= n_pro) & (tile_act == act))
        def _(act=act):
            h_ref[(i + 1) % 2, chunk_rows(jnp.clip(j - 1, 0, n_chunk - 1)), :] = normalised(x_ref)
            cast_dst_ref[...] = cast_src_ref[...].astype(BF16)
            w = w_ref[...].astype(BF16)
            o_ref[...] = _activate(jnp.dot(h_ref[i % 2], w, preferred_element_type=F32), act)

            @pl.when(i == 0)
            def _():
                os_ref[...] = _activate(jnp.dot(hs_ref[...], w, preferred_element_type=F32), act)


def _stream_norm_matmul(x, xs, g, w, cast_src, *, tile, n_chunk, cast_rows, act_ranges=()):
    tm, tn = tile
    m, k = x.shape
    ms = xs.shape[0]
    n = w.shape[1]
    n_i, n_j = m // tm, n // tn
    chunk = tm // n_chunk
    n_chunk_s = ms // chunk
    n_pro = n_chunk_s + n_chunk
    n_cast = cast_src.shape[0] // cast_rows
    assert m % tm == 0 and n % tn == 0 and tm % n_chunk == 0 and n_chunk < n_j
    assert ms % chunk == 0 and ms <= tm and cast_src.shape[0] % cast_rows == 0 and n_cast <= n_i * n_j

    def steps(s):
        step = jnp.maximum(s - n_pro, 0)
        return step // n_j, step % n_j

    def x_index(s):
        i, j = steps(s)
        nxt = jnp.minimum((i + 1) * n_chunk + jnp.clip(j - 1, 0, n_chunk - 1), n_i * n_chunk - 1)
        return jnp.where(s < n_pro, jnp.maximum(s - n_chunk_s, 0), nxt), 0

    def os_index(s):
        i, j = steps(s)
        return 0, jnp.where(i == 0, j, n_j - 1)

    cast_index = lambda s: (jnp.minimum(jnp.maximum(s - n_pro, 0), n_cast - 1), 0)
    kernel = functools.partial(_stream_norm_matmul_kernel, n_j=n_j, n_chunk=n_chunk, n_chunk_s=n_chunk_s,
                               chunk=chunk, act_ranges=act_ranges)
    return pl.pallas_call(
        kernel,
        grid=(n_pro + n_i * n_j,),
        in_specs=[
            pl.BlockSpec((chunk, k), x_index),
            pl.BlockSpec((chunk, k), lambda s: (jnp.minimum(s, n_chunk_s - 1), 0)),
            pl.BlockSpec((1, k), lambda s: (0, 0)),
            pl.BlockSpec((k, tn), lambda s: (0, steps(s)[1])),
            pl.BlockSpec((cast_rows, cast_src.shape[1]), cast_index),
        ],
        out_specs=[
            pl.BlockSpec((tm, tn), lambda s: steps(s)),
            pl.BlockSpec((ms, tn), os_index),
            pl.BlockSpec((cast_rows, cast_src.shape[1]), cast_index),
        ],
        out_shape=[
            jax.ShapeDtypeStruct((m, n), F32),
            jax.ShapeDtypeStruct((ms, n), F32),
            jax.ShapeDtypeStruct(cast_src.shape, BF16),
        ],
        scratch_shapes=[pltpu.VMEM((2, tm, k), BF16), pltpu.VMEM((ms, k), BF16)],
        compiler_params=_compiler_params(("arbitrary",)),
        name="stream_norm_matmul",
    )(x, xs, g, w, cast_src)


def _norm_matmul_pair(x, g, w_a, w_b, *, tile):
    tm, tn = tile
    m, k = x.shape
    n = w_a.shape[1]
    assert m % tm == 0 and n % tn == 0 and w_b.shape == w_a.shape
    w_spec = pl.BlockSpec((k, tn), lambda i, j: (0, j))
    o_spec = pl.BlockSpec((tm, tn), lambda i, j: (i, j))
    return pl.pallas_call(
        _norm_matmul_pair_kernel,
        grid=(m // tm, n // tn),
        in_specs=[pl.BlockSpec((tm, k), lambda i, j: (i, 0)), pl.BlockSpec((1, k), lambda i, j: (0, 0)), w_spec, w_spec],
        out_specs=[o_spec, o_spec],
        out_shape=[jax.ShapeDtypeStruct((m, n), F32)] * 2,
        scratch_shapes=[pltpu.VMEM((tm, k), BF16)],
        compiler_params=_compiler_params(("arbitrary", "arbitrary")),
        name="norm_matmul_pair",
    )(x, g, w_a, w_b)


POOL_HALO = 16
CONV_HALO = 32


def _pool_tile(extp_ref, pooled_ref, tt, d_pool, t_idx):
    pg = d_pool // len(POOL_WINDOWS)
    row = lax.broadcasted_iota(jnp.int32, (POOL_CHUNK, LANES), 0)

    for base in range(0, tt, POOL_CHUNK):
        n_prev = t_idx * tt + base + row + 1
        for gi, w in enumerate(POOL_WINDOWS):
            inv_cnt = 1.0 / jnp.minimum(w, n_prev).astype(F32)
            for cb in range(gi * pg // LANES, (gi + 1) * pg // LANES):
                cur = extp_ref[cb, base + POOL_HALO:base + POOL_HALO + POOL_CHUNK, :]
                s = cur
                for k in range(1, w):
                    s = s + extp_ref[cb, base + POOL_HALO - k:base + POOL_HALO - k + POOL_CHUNK, :]
                pooled_ref[base:base + POOL_CHUNK, cb * LANES:(cb + 1) * LANES] = s * inv_cnt - cur


def _dwconv_tile(extc_ref, wdw_ref, bdw_ref, y_ref, tt, col_blocks):
    first = CONV_HALO - (CONV_WIDTH - 1)
    for cb in col_blocks:
        c0, c1 = cb * LANES, (cb + 1) * LANES
        for base in range(0, tt, CONV_CHUNK):
            acc = jnp.broadcast_to(bdw_ref[:, c0:c1], (CONV_CHUNK, LANES))
            for k in range(CONV_WIDTH):
                acc = acc + extc_ref[cb, base + first + k:base + first + k + CONV_CHUNK, :] * wdw_ref[k:k + 1, c0:c1]
            y_ref[base:base + CONV_CHUNK, c0:c1] = acc


def _layernorm(y, g, b):
    mu = jnp.mean(y, axis=-1, keepdims=True)
    d = y - mu
    var = jnp.mean(d * d, axis=-1, keepdims=True)
    return d * lax.rsqrt(var + EPS) * g + b


def _attend(q, k, v, xhead_dim):
    s = lax.dot_general(q.astype(BF16), k, (((1,), (1,)), ((), ())), preferred_element_type=F32)
    s = s * (xhead_dim ** -0.5)
    e = jnp.exp(s - jnp.max(s, axis=-1, keepdims=True))
    l = jnp.sum(e, axis=-1, keepdims=True)
    return jnp.dot(e.astype(BF16), v, preferred_element_type=F32) / l


def _prompt_mixer_kernel(z_ref, k_ref, v_ref, wpool_ref, pscale_ref, wdw_ref, bdw_ref, lng_ref, lnb_ref, wpw_ref,
                         o_ref, pstate_ref, cstate_ref, extp_ref, extc_ref, pooled_ref, y_ref,
                         *, tt, d_pool, d_conv, d_xattn):
    t_idx = pl.program_id(1)
    n_t = pl.num_programs(1)
    pg = d_pool // len(POOL_WINDOWS)
    xhead_dim = d_xattn // N_XHEADS
    off_gate_a = d_pool
    off_val = 2 * d_pool
    off_glu = off_val + d_conv
    off_gate_b = off_glu + d_conv
    off_q = off_gate_b + d_conv
    off_gate_c = off_q + d_xattn

    @pl.when(t_idx == 0)
    def _():
        extp_ref[:, 0:POOL_HALO, :] = jnp.zeros((d_pool // LANES, POOL_HALO, LANES), F32)
        extc_ref[:, 0:CONV_HALO, :] = jnp.zeros((d_conv // LANES, CONV_HALO, LANES), F32)

    for cb in range(d_pool // LANES):
        extp_ref[cb, POOL_HALO:POOL_HALO + tt, :] = z_ref[:, cb * LANES:(cb + 1) * LANES]
    _pool_tile(extp_ref, pooled_ref, tt, d_pool, t_idx)
    for gi in range(len(POOL_WINDOWS)):
        c0, c1 = gi * pg, (gi + 1) * pg
        mixed = jnp.dot(pooled_ref[:, c0:c1].astype(BF16), wpool_ref[gi], preferred_element_type=F32)
        mixed = mixed * pscale_ref[:, c0:c1]
        o_ref[:, c0:c1] = (mixed * z_ref[:, off_gate_a + c0:off_gate_a + c1]).astype(BF16)

    for cb in range(d_conv // LANES):
        c0, c1 = cb * LANES, (cb + 1) * LANES
        extc_ref[cb, CONV_HALO:CONV_HALO + tt, :] = z_ref[:, off_val + c0:off_val + c1] * z_ref[:, off_glu + c0:off_glu + c1]
    _dwconv_tile(extc_ref, wdw_ref, bdw_ref, y_ref, tt, range(d_conv // LANES))
    act = _silu(_layernorm(y_ref[...], lng_ref[...], lnb_ref[...]))
    o_b = jnp.dot(act.astype(BF16), wpw_ref[...], preferred_element_type=F32)
    o_ref[:, d_pool:d_pool + d_conv] = (o_b * z_ref[:, off_gate_b:off_gate_b + d_conv]).astype(BF16)

    for h in range(N_XHEADS):
        c0, c1 = h * xhead_dim, (h + 1) * xhead_dim
        o_c = _attend(z_ref[:, off_q + c0:off_q + c1], k_ref[0, :, c0:c1], v_ref[0, :, c0:c1], xhead_dim)
        o_c = o_c * z_ref[:, off_gate_c + c0:off_gate_c + c1]
        o_ref[:, d_pool + d_conv + c0:d_pool + d_conv + c1] = o_c.astype(BF16)

    @pl.when(t_idx == n_t - 1)
    def _():
        for cb in range(d_pool // LANES):
            pstate_ref[0, :, cb * LANES:(cb + 1) * LANES] = extp_ref[cb, tt + 1:tt + POOL_HALO, :]
        for cb in range(d_conv // LANES):
            cstate_ref[0, :, cb * LANES:(cb + 1) * LANES] = extc_ref[cb, tt + 2:tt + CONV_HALO, :]

    extp_ref[:, 0:POOL_HALO, :] = extp_ref[:, tt:tt + POOL_HALO, :]
    extc_ref[:, 0:CONV_HALO, :] = extc_ref[:, tt:tt + CONV_HALO, :]


def _prompt_mixer(z, k, v, w_pool, pool_scale, w_dw, b_dw, ln_g, ln_b, w_pw, *, batch, seq, tt, d_model):
    d_in = z.shape[1]
    d_pool = w_pool.shape[0] * w_pool.shape[1]
    d_conv = w_pw.shape[0]
    d_xattn = k.shape[2]
    n_mem = k.shape[1]
    n_t = seq // tt
    assert seq % tt == 0 and tt % CONV_CHUNK == 0 and tt % POOL_CHUNK == 0
    kernel = functools.partial(_prompt_mixer_kernel, tt=tt, d_pool=d_pool, d_conv=d_conv, d_xattn=d_xattn)
    const2 = lambda b, t: (0, 0)
    return pl.pallas_call(
        kernel,
        grid=(batch, n_t),
        in_specs=[
            pl.BlockSpec((tt, d_in), lambda b, t: (b * n_t + t, 0)),
            pl.BlockSpec((1, n_mem, d_xattn), lambda b, t: (b, 0, 0)),
            pl.BlockSpec((1, n_mem, d_xattn), lambda b, t: (b, 0, 0)),
            pl.BlockSpec(w_pool.shape, lambda b, t: (0, 0, 0)),
            pl.BlockSpec((1, d_pool), const2),
            pl.BlockSpec(w_dw.shape, const2),
            pl.BlockSpec((1, d_conv), const2),
            pl.BlockSpec((1, d_conv), const2),
            pl.BlockSpec((1, d_conv), const2),
            pl.BlockSpec(w_pw.shape, const2),
        ],
        out_specs=[
            pl.BlockSpec((tt, d_model), lambda b, t: (b * n_t + t, 0)),
            pl.BlockSpec((1, POOL_HALO - 1, d_pool), lambda b, t: (b, 0, 0)),
            pl.BlockSpec((1, CONV_HALO - 2, d_conv), lambda b, t: (b, 0, 0)),
        ],
        out_shape=[
            jax.ShapeDtypeStruct((batch * seq, d_model), BF16),
            jax.ShapeDtypeStruct((batch, POOL_HALO - 1, d_pool), F32),
            jax.ShapeDtypeStruct((batch, CONV_HALO - 2, d_conv), F32),
        ],
        scratch_shapes=[
            pltpu.VMEM((d_pool // LANES, POOL_HALO + tt, LANES), F32),
            pltpu.VMEM((d_conv // LANES, CONV_HALO + tt, LANES), F32),
            pltpu.VMEM((tt, d_pool), F32),
            pltpu.VMEM((tt, d_conv), F32),
        ],
        compiler_params=_compiler_params(("arbitrary", "arbitrary")),
        name="prompt_mixer",
    )(z, k, v, w_pool, pool_scale, w_dw, b_dw, ln_g, ln_b, w_pw)


def _sample_mixer_kernel(z_ref, pst_ref, cst_ref, wpool_ref, pscale_ref, wdw_ref, bdw_ref, lng_ref, lnb_ref, wpw_ref,
                         o_ref, pnew_ref, cnew_ref, *, dec_seq, d_pool, d_conv):
    pg = d_pool // len(POOL_WINDOWS)
    n_pst = pst_ref.shape[0]
    n_cst = cst_ref.shape[0]
    gb = z_ref.shape[1]
    off_gate_a = d_pool
    off_val = 2 * d_pool
    off_glu = off_val + d_conv
    off_gate_b = off_glu + d_conv

    def pool_row(r, c0, c1):
        if r < n_pst:
            return pst_ref[r, :, c0:c1]
        return z_ref[r - n_pst, :, c0:c1]

    for gi, w in enumerate(POOL_WINDOWS):
        c0, c1 = gi * pg, (gi + 1) * pg
        pooled = []
        for t in range(dec_seq):
            s = pool_row(n_pst + t, c0, c1)
            cur = s
            for k in range(1, w):
                s = s + pool_row(n_pst + t - k, c0, c1)
            cnt = float(min(w, PAST_LEN + t + 1))
            pooled.append((s / cnt - cur).astype(BF16))
        mixed = jnp.dot(jnp.concatenate(pooled, axis=0), wpool_ref[gi], preferred_element_type=F32)
        mixed = mixed * pscale_ref[:, c0:c1]
        for t in range(dec_seq):
            gate = z_ref[t, :, off_gate_a + c0:off_gate_a + c1]
            o_ref[t, :, c0:c1] = (mixed[t * gb:(t + 1) * gb] * gate).astype(BF16)
    for r in range(n_pst):
        pnew_ref[r] = pool_row(r + dec_seq, 0, d_pool)

    cw = 2 * LANES
    ys = [[] for _ in range(dec_seq)]
    for cb in range(d_conv // cw):
        c0, c1 = cb * cw, (cb + 1) * cw
        a_new = [z_ref[t, :, off_val + c0:off_val + c1] * z_ref[t, :, off_glu + c0:off_glu + c1]
                 for t in range(dec_seq)]

        def conv_row(r, c0=c0, c1=c1, a_new=a_new):
            if r < n_cst:
                return cst_ref[r, :, c0:c1]
            return a_new[r - n_cst]

        acc = [jnp.broadcast_to(bdw_ref[:, c0:c1], (gb, cw)) for _ in range(dec_seq)]
        for r in range(n_cst + dec_seq):
            x_r = conv_row(r)
            for t in range(dec_seq):
                k = r - t
                if 0 <= k < CONV_WIDTH:
                    acc[t] = acc[t] + x_r * wdw_ref[k:k + 1, c0:c1]
            if r >= dec_seq:
                cnew_ref[r - dec_seq, :, c0:c1] = x_r
        for t in range(dec_seq):
            ys[t].append(acc[t])
    y = jnp.concatenate([jnp.concatenate(ys[t], axis=1) for t in range(dec_seq)], axis=0)
    act = _silu(_layernorm(y, lng_ref[...], lnb_ref[...]))
    o_b = jnp.dot(act.astype(BF16), wpw_ref[...], preferred_element_type=F32)
    for t in range(dec_seq):
        gate = z_ref[t, :, off_gate_b:off_gate_b + d_conv]
        o_ref[t, :, d_pool:d_pool + d_conv] = (o_b[t * gb:(t + 1) * gb] * gate).astype(BF16)


def _sample_mixer(z3, pstate, cstate, w_pool, pool_scale, w_dw, b_dw, ln_g, ln_b, w_pw, *, gb):
    dec_seq, dec_batch, d_in = z3.shape
    d_pool = pstate.shape[2]
    d_conv = cstate.shape[2]
    assert dec_batch % gb == 0
    kernel = functools.partial(_sample_mixer_kernel, dec_seq=dec_seq, d_pool=d_pool, d_conv=d_conv)
    const2 = lambda i: (0, 0)
    seq_block = lambda a: pl.BlockSpec((a.shape[0], gb, a.shape[2]), lambda i: (0, i, 0))
    return pl.pallas_call(
        kernel,
        grid=(dec_batch // gb,),
        in_specs=[
            seq_block(z3),
            seq_block(pstate),
            seq_block(cstate),
            pl.BlockSpec(w_pool.shape, lambda i: (0, 0, 0)),
            pl.BlockSpec((1, d_pool), const2),
            pl.BlockSpec(w_dw.shape, const2),
            pl.BlockSpec((1, d_conv), const2),
            pl.BlockSpec((1, d_conv), const2),
            pl.BlockSpec((1, d_conv), const2),
            pl.BlockSpec(w_pw.shape, const2),
        ],
        out_specs=[
            pl.BlockSpec((dec_seq, gb, d_pool + d_conv), lambda i: (0, i, 0)),
            seq_block(pstate),
            seq_block(cstate),
        ],
        out_shape=[
            jax.ShapeDtypeStruct((dec_seq, dec_batch, d_pool + d_conv), BF16),
            jax.ShapeDtypeStruct(pstate.shape, F32),
            jax.ShapeDtypeStruct(cstate.shape, F32),
        ],
        compiler_params=_compiler_params(("arbitrary",)),
        name="sample_mixer",
    )(z3, pstate, cstate, w_pool, pool_scale, w_dw, b_dw, ln_g, ln_b, w_pw)


def _sample_attn_kernel(q_ref, gate_ref, k_ref, v_ref, o_ref, *, scale):
    gb, n_rows, _ = k_ref.shape
    n_q = q_ref.shape[1]
    half_rows = n_q // 2
    col = lax.broadcasted_iota(jnp.int32, (n_q, n_rows), 1)
    row = lax.broadcasted_iota(jnp.int32, (n_q, n_rows), 0)
    match = (((col >> 2) & 1) == (row >> 4)) & ((col & 3) == ((row >> 2) & 3))
    valid = match[0:half_rows]
    for b in range(gb):
        k = k_ref[b].astype(BF16)
        v = v_ref[b].astype(BF16)
        p = lax.dot_general(q_ref[b].astype(BF16), k, (((1,), (1,)), ((), ())), preferred_element_type=F32)
        p = jnp.where(match, p, 0.0)
        s = p[0:half_rows] + pltpu.roll(p[half_rows:n_q], n_rows - 4, axis=1)
        s = jnp.where(valid, s * scale, -1e30)
        e = jnp.exp(s - jnp.max(s, axis=-1, keepdims=True))
        e = jnp.where(valid, e, 0.0)
        l = jnp.sum(e, axis=-1, keepdims=True)
        e2 = jnp.concatenate([e, pltpu.roll(e, 4, axis=1)], axis=0)
        o = jnp.dot(e2.astype(BF16), v, preferred_element_type=F32) / jnp.concatenate([l, l], axis=0)
        o_ref[b] = o * gate_ref[b]


def _sample_attn(qm, gm, cache_k, cache_v, *, gb, xhead_dim):
    nb, n_q, lanes = qm.shape
    n_rows = cache_k.shape[1]
    assert n_q == 2 * N_XHEADS * 4 and N_XHEADS == 4 and lanes == LANES
    kernel = functools.partial(_sample_attn_kernel, scale=xhead_dim ** -0.5)
    small = pl.BlockSpec((gb, n_q, lanes), lambda i: (i, 0, 0))
    big = pl.BlockSpec((gb, n_rows, lanes), lambda i: (i, 0, 0))
    return pl.pallas_call(
        kernel,
        grid=(nb // gb,),
        in_specs=[small, small, big, big],
        out_specs=small,
        out_shape=jax.ShapeDtypeStruct((nb, n_q, lanes), F32),
        compiler_params=_compiler_params(("arbitrary",)),
        name="sample_attn",
    )(qm, gm, cache_k, cache_v)


def _cache_rows(c):
    nb, n_mem, n_heads, e = c.shape
    assert n_heads == N_XHEADS and e == 2 * LANES
    return c.reshape(nb, n_mem, n_heads, 2, LANES).transpose(0, 1, 3, 2, 4).reshape(nb, n_mem * 2 * n_heads, LANES)


def _query_rows(q):
    t, nb, _ = q.shape
    return q.reshape(t, nb, N_XHEADS, 2, LANES).transpose(1, 3, 2, 0, 4).reshape(nb, 2 * N_XHEADS * t, LANES)


def _query_rows_inv(o, t):
    nb = o.shape[0]
    return o.reshape(nb, 2, N_XHEADS, t, LANES).transpose(3, 0, 2, 1, 4).reshape(t, nb, N_XHEADS * 2 * LANES)


def _out_proj_kernel(o_ref, w_ref, x_ref, g_ref, y_hbm, acc_ref, ssq_ref, sem, *, n_i, n_j, tm, tn, d_model):
    i = pl.program_id(0)
    j = pl.program_id(1)

    def slab_copy(jj, row_tile):
        dst = y_hbm.at[pl.ds(pl.multiple_of(row_tile * tm, tm), tm), jj * tn:(jj + 1) * tn]
        return pltpu.make_async_copy(acc_ref.at[jj], dst, sem.at[jj])

    part = x_ref[...] + jnp.dot(o_ref[...], w_ref[...], preferred_element_type=F32)
    row_ssq = jnp.sum(part * part, axis=-1, keepdims=True)

    @pl.when(j == 0)
    def _():
        ssq_ref[...] = row_ssq

    @pl.when(j > 0)
    def _():
        ssq_ref[...] += row_ssq

    for jj in range(n_j):

        @pl.when((j == jj) & (i > 0))
        def _(jj=jj):
            slab_copy(jj, i - 1).wait()

    acc_ref[j] = part

    @pl.when(j == n_j - 1)
    def _():
        scale = lax.rsqrt(ssq_ref[...] * (1.0 / d_model) + EPS)
        for jj in range(n_j):
            acc_ref[jj] = acc_ref[jj] * scale * g_ref[:, jj * tn:(jj + 1) * tn]
            slab_copy(jj, i).start()

        @pl.when(i == n_i - 1)
        def _():
            for jj in range(n_j):
                slab_copy(jj, i).wait()


def _out_proj(o, w, x, g, *, tile):
    tm, tn = tile
    m, k = o.shape
    n = w.shape[1]
    n_j = n // tn
    assert m % tm == 0 and n % tn == 0
    n_i = m // tm
    kernel = functools.partial(_out_proj_kernel, n_i=n_i, n_j=n_j, tm=tm, tn=tn, d_model=n)
    return pl.pallas_call(
        kernel,
        grid=(n_i, n_j),
        in_specs=[
            pl.BlockSpec((tm, k), lambda i, j: (i, 0)),
            pl.BlockSpec((k, tn), lambda i, j: (0, j)),
            pl.BlockSpec((tm, tn), lambda i, j: (i, j)),
            pl.BlockSpec((1, n), lambda i, j: (0, 0)),
        ],
        out_specs=pl.BlockSpec(memory_space=pl.ANY),
        out_shape=jax.ShapeDtypeStruct((m, n), F32),
        scratch_shapes=[pltpu.VMEM((n_j, tm, tn), F32), pltpu.VMEM((tm, 1), F32), pltpu.SemaphoreType.DMA((n_j,))],
        compiler_params=_compiler_params(("arbitrary", "arbitrary")),
        name="out_proj",
    )(o, w, x, g)


def kernel(x_prompt, mem_prompt, x_sample, cache_mem_k, cache_mem_v, state_pool, state_conv, norm_g, mem_norm_g, w_in,
           w_mem_k, w_mem_v, w_pool, pool_scale, w_dw, b_dw, conv_ln_g, conv_ln_b, w_pw, w_out, final_norm_g):
    depth = w_in.shape[0]
    assert depth == 1, "single-layer step"
    batch, seq, d_model = x_prompt.shape
    dec_batch, dec_seq, _ = x_sample.shape
    n_mem = mem_prompt.shape[1]
    d_xattn = w_mem_k.shape[2]
    d_pool = pool_scale.shape[1]
    d_conv = w_pw.shape[1]
    xhead_dim = d_xattn // N_XHEADS
    off_q = 2 * d_pool + 3 * d_conv
    l = 0

    w_pool_b = w_pool[l].astype(BF16)
    w_pw_b = w_pw[l].astype(BF16)
    g_in = norm_g[l][None, :]
    g_mem = mem_norm_g[l][None, :]
    g_fin = final_norm_g[None, :]
    pscale = pool_scale[l][None, :]
    bdw = b_dw[l][None, :]
    lng = conv_ln_g[l][None, :]
    lnb = conv_ln_b[l][None, :]
    mixer_w = (w_pool_b, pscale, w_dw[l], bdw, lng, lnb, w_pw_b)
    z_segments = ((d_pool, ACT_NONE), (d_pool, ACT_SILU), (d_conv, ACT_NONE), (d_conv, ACT_SIGMOID),
                  (d_conv, ACT_SILU), (d_xattn, ACT_NONE), (d_xattn, ACT_SILU))

    xp = x_prompt.reshape(batch * seq, d_model)
    mem = mem_prompt.reshape(batch * n_mem, d_model)
    k_p, v_p = _norm_matmul_pair(mem, g_mem, w_mem_k[l], w_mem_v[l], tile=MEM_PROJ_TILE)
    k_p = k_p.reshape(batch, n_mem, d_xattn)
    v_p = v_p.reshape(batch, n_mem, d_xattn)
    xs = x_sample.transpose(1, 0, 2).reshape(dec_seq * dec_batch, d_model)
    z_p, z_s, w_out_b = _stream_norm_matmul(
        xp, xs, g_in, w_in[l], w_out[l], tile=PROMPT_PROJ_TILE, n_chunk=PROMPT_PROJ_NORM_CHUNKS,
        cast_rows=W_OUT_CAST_ROWS, act_ranges=_act_tile_ranges(z_segments, PROMPT_PROJ_TILE[1]))
    o_p, pool_p, conv_p = _prompt_mixer(z_p, k_p.astype(BF16), v_p.astype(BF16), *mixer_w,
                                        batch=batch, seq=seq, tt=PROMPT_MIXER_ROWS, d_model=d_model)
    y_p = _out_proj(o_p, w_out_b, xp, g_fin, tile=OUT_PROJ_TILE).reshape(batch, seq, d_model)

    z_s = z_s.reshape(dec_seq, dec_batch, -1)
    o_ab, pool_s, conv_s = _sample_mixer(z_s, state_pool[l].transpose(1, 0, 2), state_conv[l].transpose(1, 0, 2),
                                         *mixer_w, gb=SAMPLE_MIXER_SEQS)
    o_c = _sample_attn(_query_rows(z_s[:, :, off_q:off_q + d_xattn]), _query_rows(z_s[:, :, off_q + d_xattn:]),
                       _cache_rows(cache_mem_k[l]), _cache_rows(cache_mem_v[l]),
                       gb=SAMPLE_ATTN_SEQS, xhead_dim=xhead_dim)
    o_c = _query_rows_inv(o_c, dec_seq)
    o_s = jnp.concatenate([o_ab, o_c.astype(BF16)], axis=-1).reshape(dec_seq * dec_batch, d_model)
    y_s = _out_proj(o_s, w_out_b, xs, g_fin, tile=SAMPLE_OUT_PROJ_TILE)
    y_s = y_s.reshape(dec_seq, dec_batch, d_model).transpose(1, 0, 2)

    mem_shape = (depth, batch, n_mem, N_XHEADS, xhead_dim)
    return (y_p, y_s, k_p.reshape(mem_shape), v_p.reshape(mem_shape), pool_p[None], conv_p[None],
            pool_s.transpose(1, 0, 2)[None], conv_s.transpose(1, 0, 2)[None])
```

```python
import functools

import jax
import jax.numpy as jnp
from jax import lax
from jax.experimental import pallas as pl
from jax.experimental.pallas import tpu as pltpu

EPS = 1e-6
POOL_WINDOWS = (2, 4, 8, 16)
CONV_WIDTH = 31
N_XHEADS = 4
PAST_LEN = 16384

SUBLANES = 8
LANES = 128
VMEM_LIMIT_BYTES = 56 * 1024 * 1024

PROMPT_PROJ_TILE = (1024, 512)
PROMPT_PROJ_NORM_CHUNKS = 16
W_OUT_CAST_ROWS = 64
MEM_PROJ_TILE = (512, 256)
OUT_PROJ_TILE = (1024, 512)
SAMPLE_OUT_PROJ_TILE = (512, 512)
PROMPT_MIXER_ROWS = 256
POOL_CHUNK = 32
CONV_CHUNK = 64
SAMPLE_MIXER_SEQS = 16
SAMPLE_ATTN_SEQS = 8

BF16 = jnp.bfloat16
F32 = jnp.float32

ACT_NONE, ACT_SILU, ACT_SIGMOID = 0, 1, 2


def _sigmoid(x):
    return 0.5 * jnp.tanh(0.5 * x) + 0.5


def _silu(x):
    return x * _sigmoid(x)


def _activate(r, act):
    if act == ACT_SILU:
        return _silu(r)
    if act == ACT_SIGMOID:
        return _sigmoid(r)
    return r


def _rms_scale(x):
    return lax.rsqrt(jnp.mean(x * x, axis=-1, keepdims=True) + EPS)


def _compiler_params(semantics):
    return pltpu.CompilerParams(dimension_semantics=semantics, vmem_limit_bytes=VMEM_LIMIT_BYTES)


def _act_tile_ranges(segments, tn):
    ranges, off = [], 0
    for width, act in segments:
        assert off % tn == 0 and width % tn == 0, "segments must be whole column tiles"
        if act != ACT_NONE:
            ranges.append((off // tn, (off + width) // tn, act))
        off += width
    return tuple(ranges)


def _tile_act(j, act_ranges):
    act = jnp.int32(ACT_NONE)
    for lo, hi, a in act_ranges:
        act = jnp.where((j >= lo) & (j < hi), a, act)
    return act


def _acts_used(act_ranges):
    return sorted({ACT_NONE} | {a for _, _, a in act_ranges})


def _norm_matmul_pair_kernel(x_ref, g_ref, wa_ref, wb_ref, oa_ref, ob_ref, h_ref):
    @pl.when(pl.program_id(1) == 0)
    def _():
        x = x_ref[...]
        h_ref[...] = (x * _rms_scale(x) * g_ref[...]).astype(BF16)

    oa_ref[...] = jnp.dot(h_ref[...], wa_ref[...].astype(BF16), preferred_element_type=F32)
    ob_ref[...] = jnp.dot(h_ref[...], wb_ref[...].astype(BF16), preferred_element_type=F32)


def _stream_norm_matmul_kernel(x_ref, xs_ref, g_ref, w_ref, cast_src_ref, o_ref, os_ref, cast_dst_ref, h_ref, hs_ref,
                               *, n_j, n_chunk, n_chunk_s, chunk, act_ranges):
    s = pl.program_id(0)
    n_pro = n_chunk_s + n_chunk
    step = jnp.maximum(s - n_pro, 0)
    i = step // n_j
    j = step % n_j

    def normalised(src_ref):
        x = src_ref[...]
        return (x * _rms_scale(x) * g_ref[...]).astype(BF16)

    def chunk_rows(c):
        return pl.ds(pl.multiple_of(c * chunk, chunk), chunk)

    @pl.when(s < n_chunk_s)
    def _():
        hs_ref[chunk_rows(s), :] = normalised(xs_ref)

    @pl.when((s >= n_chunk_s) & (s < n_pro))
    def _():
        h_ref[0, chunk_rows(s - n_chunk_s), :] = normalised(x_ref)

    tile_act = _tile_act(j, act_ranges)
    for act in _acts_used(act_ranges):

        @pl.when((s >= n_pro) & (tile_act == act))
        def _(act=act):
            h_ref[(i + 1) % 2, chunk_rows(jnp.clip(j - 1, 0, n_chunk - 1)), :] = normalised(x_ref)
            cast_dst_ref[...] = cast_src_ref[...].astype(BF16)
            w = w_ref[...].astype(BF16)
            o_ref[...] = _activate(jnp.dot(h_ref[i % 2], w, preferred_element_type=F32), act)

            @pl.when(i == 0)
            def _():
                os_ref[...] = _activate(jnp.dot(hs_ref[...], w, preferred_element_type=F32), act)


def _stream_norm_matmul(x, xs, g, w, cast_src, *, tile, n_chunk, cast_rows, act_ranges=()):
    tm, tn = tile
    m, k = x.shape
    ms = xs.shape[0]
    n = w.shape[1]
    n_i, n_j = m // tm, n // tn
    chunk = tm // n_chunk
    n_chunk_s = ms // chunk
    n_pro = n_chunk_s + n_chunk
    n_cast = cast_src.shape[0] // cast_rows
    assert m % tm == 0 and n % tn == 0 and tm % n_chunk == 0 and n_chunk < n_j
    assert ms % chunk == 0 and ms <= tm and cast_src.shape[0] % cast_rows == 0 and n_cast <= n_i * n_j

    def steps(s):
        step = jnp.maximum(s - n_pro, 0)
        return step // n_j, step % n_j

    def x_index(s):
        i, j = steps(s)
        nxt = jnp.minimum((i + 1) * n_chunk + jnp.clip(j - 1, 0, n_chunk - 1), n_i * n_chunk - 1)
        return jnp.where(s < n_pro, jnp.maximum(s - n_chunk_s, 0), nxt), 0

    def os_index(s):
        i, j = steps(s)
        return 0, jnp.where(i == 0, j, n_j - 1)

    cast_index = lambda s: (jnp.minimum(jnp.maximum(s - n_pro, 0), n_cast - 1), 0)
    kernel = functools.partial(_stream_norm_matmul_kernel, n_j=n_j, n_chunk=n_chunk, n_chunk_s=n_chunk_s,
                               chunk=chunk, act_ranges=act_ranges)
    return pl.pallas_call(
        kernel,
        grid=(n_pro + n_i * n_j,),
        in_specs=[
            pl.BlockSpec((chunk, k), x_index),
            pl.BlockSpec((chunk, k), lambda s: (jnp.minimum(s, n_chunk_s - 1), 0)),
            pl.BlockSpec((1, k), lambda s: (0, 0)),
            pl.BlockSpec((k, tn), lambda s: (0, steps(s)[1])),
            pl.BlockSpec((cast_rows, cast_src.shape[1]), cast_index),
        ],
        out_specs=[
            pl.BlockSpec((tm, tn), lambda s: steps(s)),
            pl.BlockSpec((ms, tn), os_index),
            pl.BlockSpec((cast_rows, cast_src.shape[1]), cast_index),
        ],
        out_shape=[
            jax.ShapeDtypeStruct((m, n), F32),
            jax.ShapeDtypeStruct((ms, n), F32),
            jax.ShapeDtypeStruct(cast_src.shape, BF16),
        ],
        scratch_shapes=[pltpu.VMEM((2, tm, k), BF16), pltpu.VMEM((ms, k), BF16)],
        compiler_params=_compiler_params(("arbitrary",)),
        name="stream_norm_matmul",
    )(x, xs, g, w, cast_src)


def _norm_matmul_pair(x, g, w_a, w_b, *, tile):
    tm, tn = tile
    m, k = x.shape
    n = w_a.shape[1]
    assert m % tm == 0 and n % tn == 0 and w_b.shape == w_a.shape
    w_spec = pl.BlockSpec((k, tn), lambda i, j: (0, j))
    o_spec = pl.BlockSpec((tm, tn), lambda i, j: (i, j))
    return pl.pallas_call(
        _norm_matmul_pair_kernel,
        grid=(m // tm, n // tn),
        in_specs=[pl.BlockSpec((tm, k), lambda i, j: (i, 0)), pl.BlockSpec((1, k), lambda i, j: (0, 0)), w_spec, w_spec],
        out_specs=[o_spec, o_spec],
        out_shape=[jax.ShapeDtypeStruct((m, n), F32)] * 2,
        scratch_shapes=[pltpu.VMEM((tm, k), BF16)],
        compiler_params=_compiler_params(("arbitrary", "arbitrary")),
        name="norm_matmul_pair",
    )(x, g, w_a, w_b)


POOL_HALO = 16
CONV_HALO = 32


def _pool_tile(extp_ref, pooled_ref, tt, d_pool, t_idx):
    pg = d_pool // len(POOL_WINDOWS)
    row = lax.broadcasted_iota(jnp.int32, (POOL_CHUNK, LANES), 0)

    for base in range(0, tt, POOL_CHUNK):
        n_prev = t_idx * tt + base + row + 1
        for gi, w in enumerate(POOL_WINDOWS):
            inv_cnt = 1.0 / jnp.minimum(w, n_prev).astype(F32)
            for cb in range(gi * pg // LANES, (gi + 1) * pg // LANES):
                cur = extp_ref[cb, base + POOL_HALO:base + POOL_HALO + POOL_CHUNK, :]
                s = cur
                for k in range(1, w):
                    s = s + extp_ref[cb, base + POOL_HALO - k:base + POOL_HALO - k + POOL_CHUNK, :]
                pooled_ref[base:base + POOL_CHUNK, cb * LANES:(cb + 1) * LANES] = s * inv_cnt - cur


def _dwconv_tile(extc_ref, wdw_ref, bdw_ref, y_ref, tt, col_blocks):
    first = CONV_HALO - (CONV_WIDTH - 1)
    for cb in col_blocks:
        c0, c1 = cb * LANES, (cb + 1) * LANES
        for base in range(0, tt, CONV_CHUNK):
            acc = jnp.broadcast_to(bdw_ref[:, c0:c1], (CONV_CHUNK, LANES))
            for k in range(CONV_WIDTH):
                acc = acc + extc_ref[cb, base + first + k:base + first + k + CONV_CHUNK, :] * wdw_ref[k:k + 1, c0:c1]
            y_ref[base:base + CONV_CHUNK, c0:c1] = acc


def _layernorm(y, g, b):
    mu = jnp.mean(y, axis=-1, keepdims=True)
    d = y - mu
    var = jnp.mean(d * d, axis=-1, keepdims=True)
    return d * lax.rsqrt(var + EPS) * g + b


def _attend(q, k, v, xhead_dim):
    s = lax.dot_general(q.astype(BF16), k, (((1,), (1,)), ((), ())), preferred_element_type=F32)
    s = s * (xhead_dim ** -0.5)
    e = jnp.exp(s - jnp.max(s, axis=-1, keepdims=True))
    l = jnp.sum(e, axis=-1, keepdims=True)
    return jnp.dot(e.astype(BF16), v, preferred_element_type=F32) / l


def _prompt_mixer_kernel(z_ref, k_ref, v_ref, wpool_ref, pscale_ref, wdw_ref, bdw_ref, lng_ref, lnb_ref, wpw_ref,
                         o_ref, pstate_ref, cstate_ref, extp_ref, extc_ref, pooled_ref, y_ref,
                         *, tt, d_pool, d_conv, d_xattn):
    t_idx = pl.program_id(1)
    n_t = pl.num_programs(1)
    pg = d_pool // len(POOL_WINDOWS)
    xhead_dim = d_xattn // N_XHEADS
    off_gate_a = d_pool
    off_val = 2 * d_pool
    off_glu = off_val + d_conv
    off_gate_b = off_glu + d_conv
    off_q = off_gate_b + d_conv
    off_gate_c = off_q + d_xattn

    @pl.when(t_idx == 0)
    def _():
        extp_ref[:, 0:POOL_HALO, :] = jnp.zeros((d_pool // LANES, POOL_HALO, LANES), F32)
        extc_ref[:, 0:CONV_HALO, :] = jnp.zeros((d_conv // LANES, CONV_HALO, LANES), F32)

    for cb in range(d_pool // LANES):
        extp_ref[cb, POOL_HALO:POOL_HALO + tt, :] = z_ref[:, cb * LANES:(cb + 1) * LANES]
    _pool_tile(extp_ref, pooled_ref, tt, d_pool, t_idx)
    for gi in range(len(POOL_WINDOWS)):
        c0, c1 = gi * pg, (gi + 1) * pg
        mixed = jnp.dot(pooled_ref[:, c0:c1].astype(BF16), wpool_ref[gi], preferred_element_type=F32)
        mixed = mixed * pscale_ref[:, c0:c1]
        o_ref[:, c0:c1] = (mixed * z_ref[:, off_gate_a + c0:off_gate_a + c1]).astype(BF16)

    for cb in range(d_conv // LANES):
        c0, c1 = cb * LANES, (cb + 1) * LANES
        extc_ref[cb, CONV_HALO:CONV_HALO + tt, :] = z_ref[:, off_val + c0:off_val + c1] * z_ref[:, off_glu + c0:off_glu + c1]
    _dwconv_tile(extc_ref, wdw_ref, bdw_ref, y_ref, tt, range(d_conv // LANES))
    act = _silu(_layernorm(y_ref[...], lng_ref[...], lnb_ref[...]))
    o_b = jnp.dot(act.astype(BF16), wpw_ref[...], preferred_element_type=F32)
    o_ref[:, d_pool:d_pool + d_conv] = (o_b * z_ref[:, off_gate_b:off_gate_b + d_conv]).astype(BF16)

    for h in range(N_XHEADS):
        c0, c1 = h * xhead_dim, (h + 1) * xhead_dim
        o_c = _attend(z_ref[:, off_q + c0:off_q + c1], k_ref[0, :, c0:c1].astype(BF16),
                      v_ref[0, :, c0:c1].astype(BF16), xhead_dim)
        o_c = o_c * z_ref[:, off_gate_c + c0:off_gate_c + c1]
        o_ref[:, d_pool + d_conv + c0:d_pool + d_conv + c1] = o_c.astype(BF16)

    @pl.when(t_idx == n_t - 1)
    def _():
        for cb in range(d_pool // LANES):
            pstate_ref[0, :, cb * LANES:(cb + 1) * LANES] = extp_ref[cb, tt + 1:tt + POOL_HALO, :]
        for cb in range(d_conv // LANES):
            cstate_ref[0, :, cb * LANES:(cb + 1) * LANES] = extc_ref[cb, tt + 2:tt + CONV_HALO, :]

    extp_ref[:, 0:POOL_HALO, :] = extp_ref[:, tt:tt + POOL_HALO, :]
    extc_ref[:, 0:CONV_HALO, :] = extc_ref[:, tt:tt + CONV_HALO, :]


def _prompt_mixer(z, k, v, w_pool, pool_scale, w_dw, b_dw, ln_g, ln_b, w_pw, *, batch, seq, tt, d_model):
    d_in = z.shape[1]
    d_pool = w_pool.shape[0] * w_pool.shape[1]
    d_conv = w_pw.shape[0]
    d_xattn = k.shape[2]
    n_mem = k.shape[1]
    n_t = seq // tt
    assert seq % tt == 0 and tt % CONV_CHUNK == 0 and tt % POOL_CHUNK == 0
    kernel = functools.partial(_prompt_mixer_kernel, tt=tt, d_pool=d_pool, d_conv=d_conv, d_xattn=d_xattn)
    const2 = lambda b, t: (0, 0)
    return pl.pallas_call(
        kernel,
        grid=(batch, n_t),
        in_specs=[
            pl.BlockSpec((tt, d_in), lambda b, t: (b * n_t + t, 0)),
            pl.BlockSpec((1, n_mem, d_xattn), lambda b, t: (b, 0, 0)),
            pl.BlockSpec((1, n_mem, d_xattn), lambda b, t: (b, 0, 0)),
            pl.BlockSpec(w_pool.shape, lambda b, t: (0, 0, 0)),
            pl.BlockSpec((1, d_pool), const2),
            pl.BlockSpec(w_dw.shape, const2),
            pl.BlockSpec((1, d_conv), const2),
            pl.BlockSpec((1, d_conv), const2),
            pl.BlockSpec((1, d_conv), const2),
            pl.BlockSpec(w_pw.shape, const2),
        ],
        out_specs=[
            pl.BlockSpec((tt, d_model), lambda b, t: (b * n_t + t, 0)),
            pl.BlockSpec((1, POOL_HALO - 1, d_pool), lambda b, t: (b, 0, 0)),
            pl.BlockSpec((1, CONV_HALO - 2, d_conv), lambda b, t: (b, 0, 0)),
        ],
        out_shape=[
            jax.ShapeDtypeStruct((batch * seq, d_model), BF16),
            jax.ShapeDtypeStruct((batch, POOL_HALO - 1, d_pool), F32),
            jax.ShapeDtypeStruct((batch, CONV_HALO - 2, d_conv), F32),
        ],
        scratch_shapes=[
            pltpu.VMEM((d_pool // LANES, POOL_HALO + tt, LANES), F32),
            pltpu.VMEM((d_conv // LANES, CONV_HALO + tt, LANES), F32),
            pltpu.VMEM((tt, d_pool), F32),
            pltpu.VMEM((tt, d_conv), F32),
        ],
        compiler_params=_compiler_params(("arbitrary", "arbitrary")),
        name="prompt_mixer",
    )(z, k, v, w_pool, pool_scale, w_dw, b_dw, ln_g, ln_b, w_pw)


def _sample_mixer_kernel(z_ref, pst_ref, cst_ref, wpool_ref, pscale_ref, wdw_ref, bdw_ref, lng_ref, lnb_ref, wpw_ref,
                         o_ref, pnew_ref, cnew_ref, *, dec_seq, d_pool, d_conv):
    pg = d_pool // len(POOL_WINDOWS)
    n_pst = pst_ref.shape[0]
    n_cst = cst_ref.shape[0]
    gb = z_ref.shape[1]
    off_gate_a = d_pool
    off_val = 2 * d_pool
    off_glu = off_val + d_conv
    off_gate_b = off_glu + d_conv

    def pool_row(r, c0, c1):
        if r < n_pst:
            return pst_ref[r, :, c0:c1]
        return z_ref[r - n_pst, :, c0:c1]

    for gi, w in enumerate(POOL_WINDOWS):
        c0, c1 = gi * pg, (gi + 1) * pg
        pooled = []
        for t in range(dec_seq):
            s = pool_row(n_pst + t, c0, c1)
            cur = s
            for k in range(1, w):
                s = s + pool_row(n_pst + t - k, c0, c1)
            cnt = float(min(w, PAST_LEN + t + 1))
            pooled.append((s / cnt - cur).astype(BF16))
        mixed = jnp.dot(jnp.concatenate(pooled, axis=0), wpool_ref[gi], preferred_element_type=F32)
        mixed = mixed * pscale_ref[:, c0:c1]
        for t in range(dec_seq):
            gate = z_ref[t, :, off_gate_a + c0:off_gate_a + c1]
            o_ref[t, :, c0:c1] = (mixed[t * gb:(t + 1) * gb] * gate).astype(BF16)
    for r in range(n_pst):
        pnew_ref[r] = pool_row(r + dec_seq, 0, d_pool)

    cw = 2 * LANES
    ys = [[] for _ in range(dec_seq)]
    for cb in range(d_conv // cw):
        c0, c1 = cb * cw, (cb + 1) * cw
        a_new = [z_ref[t, :, off_val + c0:off_val + c1] * z_ref[t, :, off_glu + c0:off_glu + c1]
                 for t in range(dec_seq)]

        def conv_row(r, c0=c0, c1=c1, a_new=a_new):
            if r < n_cst:
                return cst_ref[r, :, c0:c1]
            return a_new[r - n_cst]

        acc = [jnp.broadcast_to(bdw_ref[:, c0:c1], (gb, cw)) for _ in range(dec_seq)]
        for r in range(n_cst + dec_seq):
            x_r = conv_row(r)
            for t in range(dec_seq):
                k = r - t
                if 0 <= k < CONV_WIDTH:
                    acc[t] = acc[t] + x_r * wdw_ref[k:k + 1, c0:c1]
            if r >= dec_seq:
                cnew_ref[r - dec_seq, :, c0:c1] = x_r
        for t in range(dec_seq):
            ys[t].append(acc[t])
    y = jnp.concatenate([jnp.concatenate(ys[t], axis=1) for t in range(dec_seq)], axis=0)
    act = _silu(_layernorm(y, lng_ref[...], lnb_ref[...]))
    o_b = jnp.dot(act.astype(BF16), wpw_ref[...], preferred_element_type=F32)
    for t in range(dec_seq):
        gate = z_ref[t, :, off_gate_b:off_gate_b + d_conv]
        o_ref[t, :, d_pool:d_pool + d_conv] = (o_b[t * gb:(t + 1) * gb] * gate).astype(BF16)


def _sample_mixer(z3, pstate, cstate, w_pool, pool_scale, w_dw, b_dw, ln_g, ln_b, w_pw, *, gb):
    dec_seq, dec_batch, d_in = z3.shape
    d_pool = pstate.shape[2]
    d_conv = cstate.shape[2]
    assert dec_batch % gb == 0
    kernel = functools.partial(_sample_mixer_kernel, dec_seq=dec_seq, d_pool=d_pool, d_conv=d_conv)
    const2 = lambda i: (0, 0)
    seq_block = lambda a: pl.BlockSpec((a.shape[0], gb, a.shape[2]), lambda i: (0, i, 0))
    return pl.pallas_call(
        kernel,
        grid=(dec_batch // gb,),
        in_specs=[
            seq_block(z3),
            seq_block(pstate),
            seq_block(cstate),
            pl.BlockSpec(w_pool.shape, lambda i: (0, 0, 0)),
            pl.BlockSpec((1, d_pool), const2),
            pl.BlockSpec(w_dw.shape, const2),
            pl.BlockSpec((1, d_conv), const2),
            pl.BlockSpec((1, d_conv), const2),
            pl.BlockSpec((1, d_conv), const2),
            pl.BlockSpec(w_pw.shape, const2),
        ],
        out_specs=[
            pl.BlockSpec((dec_seq, gb, d_pool + d_conv), lambda i: (0, i, 0)),
            seq_block(pstate),
            seq_block(cstate),
        ],
        out_shape=[
            jax.ShapeDtypeStruct((dec_seq, dec_batch, d_pool + d_conv), BF16),
            jax.ShapeDtypeStruct(pstate.shape, F32),
            jax.ShapeDtypeStruct(cstate.shape, F32),
        ],
        compiler_params=_compiler_params(("arbitrary",)),
        name="sample_mixer",
    )(z3, pstate, cstate, w_pool, pool_scale, w_dw, b_dw, ln_g, ln_b, w_pw)


def _sample_attn_kernel(q_ref, gate_ref, k_ref, v_ref, o_ref, *, scale):
    gb, n_rows, _ = k_ref.shape
    n_q = q_ref.shape[1]
    half_rows = n_q // 2
    col = lax.broadcasted_iota(jnp.int32, (n_q, n_rows), 1)
    row = lax.broadcasted_iota(jnp.int32, (n_q, n_rows), 0)
    match = (((col >> 2) & 1) == (row >> 4)) & ((col & 3) == ((row >> 2) & 3))
    valid = match[0:half_rows]
    seqs = range(gb)
    p = [lax.dot_general(q_ref[b].astype(BF16), k_ref[b].astype(BF16), (((1,), (1,)), ((), ())),
                         preferred_element_type=F32) for b in seqs]
    p = [jnp.where(match, p[b], 0.0) for b in seqs]
    s = [p[b][0:half_rows] + pltpu.roll(p[b][half_rows:n_q], n_rows - 4, axis=1) for b in seqs]
    s = [jnp.where(valid, s[b] * scale, -1e30) for b in seqs]
    m = [jnp.max(s[b], axis=-1, keepdims=True) for b in seqs]
    e = [jnp.where(valid, jnp.exp(s[b] - m[b]), 0.0) for b in seqs]
    l = [jnp.sum(e[b], axis=-1, keepdims=True) for b in seqs]
    e2 = [jnp.concatenate([e[b], pltpu.roll(e[b], 4, axis=1)], axis=0).astype(BF16) for b in seqs]
    o = [jnp.dot(e2[b], v_ref[b].astype(BF16), preferred_element_type=F32) for b in seqs]
    for b in seqs:
        o_ref[b] = o[b] / jnp.concatenate([l[b], l[b]], axis=0) * gate_ref[b]


def _sample_attn(qm, gm, cache_k, cache_v, *, gb, xhead_dim):
    nb, n_q, lanes = qm.shape
    n_rows = cache_k.shape[1]
    assert n_q == 2 * N_XHEADS * 4 and N_XHEADS == 4 and lanes == LANES
    kernel = functools.partial(_sample_attn_kernel, scale=xhead_dim ** -0.5)
    small = pl.BlockSpec((gb, n_q, lanes), lambda i: (i, 0, 0))
    big = pl.BlockSpec((gb, n_rows, lanes), lambda i: (i, 0, 0))
    return pl.pallas_call(
        kernel,
        grid=(nb // gb,),
        in_specs=[small, small, big, big],
        out_specs=small,
        out_shape=jax.ShapeDtypeStruct((nb, n_q, lanes), F32),
        compiler_params=_compiler_params(("arbitrary",)),
        name="sample_attn",
    )(qm, gm, cache_k, cache_v)


def _cache_rows(c):
    nb, n_mem, n_heads, e = c.shape
    assert n_heads == N_XHEADS and e == 2 * LANES
    return c.reshape(nb, n_mem, n_heads, 2, LANES).transpose(0, 1, 3, 2, 4).reshape(nb, n_mem * 2 * n_heads, LANES)


def _query_rows(q):
    t, nb, _ = q.shape
    return q.reshape(t, nb, N_XHEADS, 2, LANES).transpose(1, 3, 2, 0, 4).reshape(nb, 2 * N_XHEADS * t, LANES)


def _query_rows_inv(o, t):
    nb = o.shape[0]
    return o.reshape(nb, 2, N_XHEADS, t, LANES).transpose(3, 0, 2, 1, 4).reshape(t, nb, N_XHEADS * 2 * LANES)


def _out_proj_kernel(o_ref, w_ref, x_ref, g_ref, y_hbm, acc_ref, ssq_ref, sem, *, n_i, n_j, tm, tn, d_model):
    i = pl.program_id(0)
    j = pl.program_id(1)

    def slab_copy(jj, row_tile):
        dst = y_hbm.at[pl.ds(pl.multiple_of(row_tile * tm, tm), tm), jj * tn:(jj + 1) * tn]
        return pltpu.make_async_copy(acc_ref.at[jj], dst, sem.at[jj])

    part = x_ref[...] + jnp.dot(o_ref[...], w_ref[...], preferred_element_type=F32)
    row_ssq = jnp.sum(part * part, axis=-1, keepdims=True)

    @pl.when(j == 0)
    def _():
        ssq_ref[...] = row_ssq

    @pl.when(j > 0)
    def _():
        ssq_ref[...] += row_ssq

    for jj in range(n_j):

        @pl.when((j == jj) & (i > 0))
        def _(jj=jj):
            slab_copy(jj, i - 1).wait()

    acc_ref[j] = part

    @pl.when(j == n_j - 1)
    def _():
        scale = lax.rsqrt(ssq_ref[...] * (1.0 / d_model) + EPS)
        for jj in range(n_j):
            acc_ref[jj] = acc_ref[jj] * scale * g_ref[:, jj * tn:(jj + 1) * tn]
            slab_copy(jj, i).start()

        @pl.when(i == n_i - 1)
        def _():
            for jj in range(n_j):
                slab_copy(jj, i).wait()


def _out_proj(o, w, x, g, *, tile):
    tm, tn = tile
    m, k = o.shape
    n = w.shape[1]
    n_j = n // tn
    assert m % tm == 0 and n % tn == 0
    n_i = m // tm
    kernel = functools.partial(_out_proj_kernel, n_i=n_i, n_j=n_j, tm=tm, tn=tn, d_model=n)
    return pl.pallas_call(
        kernel,
        grid=(n_i, n_j),
        in_specs=[
            pl.BlockSpec((tm, k), lambda i, j: (i, 0)),
            pl.BlockSpec((k, tn), lambda i, j: (0, j)),
            pl.BlockSpec((tm, tn), lambda i, j: (i, j)),
            pl.BlockSpec((1, n), lambda i, j: (0, 0)),
        ],
        out_specs=pl.BlockSpec(memory_space=pl.ANY),
        out_shape=jax.ShapeDtypeStruct((m, n), F32),
        scratch_shapes=[pltpu.VMEM((n_j, tm, tn), F32), pltpu.VMEM((tm, 1), F32), pltpu.SemaphoreType.DMA((n_j,))],
        compiler_params=_compiler_params(("arbitrary", "arbitrary")),
        name="out_proj",
    )(o, w, x, g)


def kernel(x_prompt, mem_prompt, x_sample, cache_mem_k, cache_mem_v, state_pool, state_conv, norm_g, mem_norm_g, w_in,
           w_mem_k, w_mem_v, w_pool, pool_scale, w_dw, b_dw, conv_ln_g, conv_ln_b, w_pw, w_out, final_norm_g):
    depth = w_in.shape[0]
    assert depth == 1, "single-layer step"
    batch, seq, d_model = x_prompt.shape
    dec_batch, dec_seq, _ = x_sample.shape
    n_mem = mem_prompt.shape[1]
    d_xattn = w_mem_k.shape[2]
    d_pool = pool_scale.shape[1]
    d_conv = w_pw.shape[1]
    xhead_dim = d_xattn // N_XHEADS
    off_q = 2 * d_pool + 3 * d_conv
    l = 0

    w_pool_b = w_pool[l].astype(BF16)
    w_pw_b = w_pw[l].astype(BF16)
    g_in = norm_g[l][None, :]
    g_mem = mem_norm_g[l][None, :]
    g_fin = final_norm_g[None, :]
    pscale = pool_scale[l][None, :]
    bdw = b_dw[l][None, :]
    lng = conv_ln_g[l][None, :]
    lnb = conv_ln_b[l][None, :]
    mixer_w = (w_pool_b, pscale, w_dw[l], bdw, lng, lnb, w_pw_b)
    z_segments = ((d_pool, ACT_NONE), (d_pool, ACT_SILU), (d_conv, ACT_NONE), (d_conv, ACT_SIGMOID),
                  (d_conv, ACT_SILU), (d_xattn, ACT_NONE), (d_xattn, ACT_SILU))

    xp = x_prompt.reshape(batch * seq, d_model)
    mem = mem_prompt.reshape(batch * n_mem, d_model)
    k_p, v_p = _norm_matmul_pair(mem, g_mem, w_mem_k[l], w_mem_v[l], tile=MEM_PROJ_TILE)
    k_p = k_p.reshape(batch, n_mem, d_xattn)
    v_p = v_p.reshape(batch, n_mem, d_xattn)
    xs = x_sample.transpose(1, 0, 2).reshape(dec_seq * dec_batch, d_model)
    z_p, z_s, w_out_b = _stream_norm_matmul(
        xp, xs, g_in, w_in[l], w_out[l], tile=PROMPT_PROJ_TILE, n_chunk=PROMPT_PROJ_NORM_CHUNKS,
        cast_rows=W_OUT_CAST_ROWS, act_ranges=_act_tile_ranges(z_segments, PROMPT_PROJ_TILE[1]))
    o_p, pool_p, conv_p = _prompt_mixer(z_p, k_p, v_p, *mixer_w,
                                        batch=batch, seq=seq, tt=PROMPT_MIXER_ROWS, d_model=d_model)
    y_p = _out_proj(o_p, w_out_b, xp, g_fin, tile=OUT_PROJ_TILE).reshape(batch, seq, d_model)

    z_s = z_s.reshape(dec_seq, dec_batch, -1)
    o_ab, pool_s, conv_s = _sample_mixer(z_s, state_pool[l].transpose(1, 0, 2), state_conv[l].transpose(1, 0, 2),
                                         *mixer_w, gb=SAMPLE_MIXER_SEQS)
    o_c = _sample_attn(_query_rows(z_s[:, :, off_q:off_q + d_xattn]), _query_rows(z_s[:, :, off_q + d_xattn:]),
                       _cache_rows(cache_mem_k[l]), _cache_rows(cache_mem_v[l]),
                       gb=SAMPLE_ATTN_SEQS, xhead_dim=xhead_dim)
    o_c = _query_rows_inv(o_c, dec_seq)
    o_s = jnp.concatenate([o_ab, o_c.astype(BF16)], axis=-1).reshape(dec_seq * dec_batch, d_model)
    y_s = _out_proj(o_s, w_out_b, xs, g_fin, tile=SAMPLE_OUT_PROJ_TILE)
    y_s = y_s.reshape(dec_seq, dec_batch, d_model).transpose(1, 0, 2)

    mem_shape = (depth, batch, n_mem, N_XHEADS, xhead_dim)
    return (y_p, y_s, k_p.reshape(mem_shape), v_p.reshape(mem_shape), pool_p[None], conv_p[None],
            pool_s.transpose(1, 0, 2)[None], conv_s.transpose(1, 0, 2)[None])
```

```python
import functools

import jax
import jax.numpy as jnp
from jax import lax
from jax.experimental import pallas as pl
from jax.experimental.pallas import tpu as pltpu

EPS = 1e-6
POOL_WINDOWS = (2, 4, 8, 16)
CONV_WIDTH = 31
N_XHEADS = 4
PAST_LEN = 16384

SUBLANES = 8
LANES = 128
VMEM_LIMIT_BYTES = 56 * 1024 * 1024

PROMPT_PROJ_TILE = (1024, 512)
PROMPT_PROJ_NORM_CHUNKS = 16
W_OUT_CAST_ROWS = 64
MEM_PROJ_TILE = (512, 256)
OUT_PROJ_TILE = (1024, 512)
SAMPLE_OUT_PROJ_TILE = (512, 512)
PROMPT_MIXER_ROWS = 256
POOL_CHUNK = 32
CONV_CHUNK = 64
SAMPLE_MIXER_SEQS = 16
SAMPLE_ATTN_SEQS = 8

BF16 = jnp.bfloat16
F32 = jnp.float32

ACT_NONE, ACT_SILU, ACT_SIGMOID = 0, 1, 2


def _sigmoid(x):
    return 0.5 * jnp.tanh(0.5 * x) + 0.5


def _silu(x):
    return x * _sigmoid(x)


def _activate(r, act):
    if act == ACT_SILU:
        return _silu(r)
    if act == ACT_SIGMOID:
        return _sigmoid(r)
    return r


def _rms_scale(x):
    return lax.rsqrt(jnp.mean(x * x, axis=-1, keepdims=True) + EPS)


def _compiler_params(semantics):
    return pltpu.CompilerParams(dimension_semantics=semantics, vmem_limit_bytes=VMEM_LIMIT_BYTES)


def _act_tile_ranges(segments, tn):
    ranges, off = [], 0
    for width, act in segments:
        assert off % tn == 0 and width % tn == 0, "segments must be whole column tiles"
        if act != ACT_NONE:
            ranges.append((off // tn, (off + width) // tn, act))
        off += width
    return tuple(ranges)


def _tile_act(j, act_ranges):
    act = jnp.int32(ACT_NONE)
    for lo, hi, a in act_ranges:
        act = jnp.where((j >= lo) & (j < hi), a, act)
    return act


def _acts_used(act_ranges):
    return sorted({ACT_NONE} | {a for _, _, a in act_ranges})


def _norm_matmul_pair_kernel(x_ref, g_ref, wa_ref, wb_ref, oa_ref, ob_ref, h_ref):
    @pl.when(pl.program_id(1) == 0)
    def _():
        x = x_ref[...]
        h_ref[...] = (x * _rms_scale(x) * g_ref[...]).astype(BF16)

    oa_ref[...] = jnp.dot(h_ref[...], wa_ref[...].astype(BF16), preferred_element_type=F32)
    ob_ref[...] = jnp.dot(h_ref[...], wb_ref[...].astype(BF16), preferred_element_type=F32)


def _stream_norm_matmul_kernel(x_ref, xs_ref, g_ref, w_ref, cast_src_ref, o_ref, os_ref, cast_dst_ref, h_ref, hs_ref,
                               *, n_j, n_chunk, n_chunk_s, chunk, act_ranges):
    s = pl.program_id(0)
    n_pro = n_chunk_s + n_chunk
    step = jnp.maximum(s - n_pro, 0)
    i = step // n_j
    j = step % n_j

    def normalised(src_ref):
        x = src_ref[...]
        return (x * _rms_scale(x) * g_ref[...]).astype(BF16)

    def chunk_rows(c):
        return pl.ds(pl.multiple_of(c * chunk, chunk), chunk)

    @pl.when(s < n_chunk_s)
    def _():
        hs_ref[chunk_rows(s), :] = normalised(xs_ref)

    @pl.when((s >= n_chunk_s) & (s < n_pro))
    def _():
        h_ref[0, chunk_rows(s - n_chunk_s), :] = normalised(x_ref)

    tile_act = _tile_act(j, act_ranges)
    for act in _acts_used(act_ranges):

        @pl.when((s >= n_pro) & (tile_act == act))
        def _(act=act):
            h_ref[(i + 1) % 2, chunk_rows(jnp.clip(j - 1, 0, n_chunk - 1)), :] = normalised(x_ref)
            cast_dst_ref[...] = cast_src_ref[...].astype(BF16)
            w = w_ref[...].astype(BF16)
            o_ref[...] = _activate(jnp.dot(h_ref[i % 2], w, preferred_element_type=F32), act)

            @pl.when(i == 0)
            def _():
                os_ref[...] = _activate(jnp.dot(hs_ref[...], w, preferred_element_type=F32), act)


def _stream_norm_matmul(x, xs, g, w, cast_src, *, tile, n_chunk, cast_rows, act_ranges=()):
    tm, tn = tile
    m, k = x.shape
    ms = xs.shape[0]
    n = w.shape[1]
    n_i, n_j = m // tm, n // tn
    chunk = tm // n_chunk
    n_chunk_s = ms // chunk
    n_pro = n_chunk_s + n_chunk
    n_cast = cast_src.shape[0] // cast_rows
    assert m % tm == 0 and n % tn == 0 and tm % n_chunk == 0 and n_chunk < n_j
    assert ms % chunk == 0 and ms <= tm and cast_src.shape[0] % cast_rows == 0 and n_cast <= n_i * n_j

    def steps(s):
        step = jnp.maximum(s - n_pro, 0)
        return step // n_j, step % n_j

    def x_index(s):
        i, j = steps(s)
        nxt = jnp.minimum((i + 1) * n_chunk + jnp.clip(j - 1, 0, n_chunk - 1), n_i * n_chunk - 1)
        return jnp.where(s < n_pro, jnp.maximum(s - n_chunk_s, 0), nxt), 0

    def os_index(s):
        i, j = steps(s)
        return 0, jnp.where(i == 0, j, n_j - 1)

    cast_index = lambda s: (jnp.minimum(jnp.maximum(s - n_pro, 0), n_cast - 1), 0)
    kernel = functools.partial(_stream_norm_matmul_kernel, n_j=n_j, n_chunk=n_chunk, n_chunk_s=n_chunk_s,
                               chunk=chunk, act_ranges=act_ranges)
    return pl.pallas_call(
        kernel,
        grid=(n_pro + n_i * n_j,),
        in_specs=[
            pl.BlockSpec((chunk, k), x_index),
            pl.BlockSpec((chunk, k), lambda s: (jnp.minimum(s, n_chunk_s - 1), 0)),
            pl.BlockSpec((1, k), lambda s: (0, 0)),
            pl.BlockSpec((k, tn), lambda s: (0, steps(s)[1])),
            pl.BlockSpec((cast_rows, cast_src.shape[1]), cast_index),
        ],
        out_specs=[
            pl.BlockSpec((tm, tn), lambda s: steps(s)),
            pl.BlockSpec((ms, tn), os_index),
            pl.BlockSpec((cast_rows, cast_src.shape[1]), cast_index),
        ],
        out_shape=[
            jax.ShapeDtypeStruct((m, n), F32),
            jax.ShapeDtypeStruct((ms, n), F32),
            jax.ShapeDtypeStruct(cast_src.shape, BF16),
        ],
        scratch_shapes=[pltpu.VMEM((2, tm, k), BF16), pltpu.VMEM((ms, k), BF16)],
        compiler_params=_compiler_params(("arbitrary",)),
        name="stream_norm_matmul",
    )(x, xs, g, w, cast_src)


def _norm_matmul_pair(x, g, w_a, w_b, *, tile):
    tm, tn = tile
    m, k = x.shape
    n = w_a.shape[1]
    assert m % tm == 0 and n % tn == 0 and w_b.shape == w_a.shape
    w_spec = pl.BlockSpec((k, tn), lambda i, j: (0, j))
    o_spec = pl.BlockSpec((tm, tn), lambda i, j: (i, j))
    return pl.pallas_call(
        _norm_matmul_pair_kernel,
        grid=(m // tm, n // tn),
        in_specs=[pl.BlockSpec((tm, k), lambda i, j: (i, 0)), pl.BlockSpec((1, k), lambda i, j: (0, 0)), w_spec, w_spec],
        out_specs=[o_spec, o_spec],
        out_shape=[jax.ShapeDtypeStruct((m, n), F32)] * 2,
        scratch_shapes=[pltpu.VMEM((tm, k), BF16)],
        compiler_params=_compiler_params(("arbitrary", "arbitrary")),
        name="norm_matmul_pair",
    )(x, g, w_a, w_b)


POOL_HALO = 16
CONV_HALO = 32


def _pool_tile(extp_ref, pooled_ref, tt, d_pool, t_idx):
    pg = d_pool // len(POOL_WINDOWS)
    row = lax.broadcasted_iota(jnp.int32, (POOL_CHUNK, LANES), 0)

    for base in range(0, tt, POOL_CHUNK):
        n_prev = t_idx * tt + base + row + 1
        for gi, w in enumerate(POOL_WINDOWS):
            inv_cnt = 1.0 / jnp.minimum(w, n_prev).astype(F32)
            for cb in range(gi * pg // LANES, (gi + 1) * pg // LANES):
                cur = extp_ref[cb, base + POOL_HALO:base + POOL_HALO + POOL_CHUNK, :]
                s = cur
                for k in range(1, w):
                    s = s + extp_ref[cb, base + POOL_HALO - k:base + POOL_HALO - k + POOL_CHUNK, :]
                pooled_ref[base:base + POOL_CHUNK, cb * LANES:(cb + 1) * LANES] = s * inv_cnt - cur


def _dwconv_tile(extc_ref, wdw_ref, bdw_ref, y_ref, tt, col_blocks):
    first = CONV_HALO - (CONV_WIDTH - 1)
    for cb in col_blocks:
        c0, c1 = cb * LANES, (cb + 1) * LANES
        for base in range(0, tt, CONV_CHUNK):
            acc = jnp.broadcast_to(bdw_ref[:, c0:c1], (CONV_CHUNK, LANES))
            for k in range(CONV_WIDTH):
                acc = acc + extc_ref[cb, base + first + k:base + first + k + CONV_CHUNK, :] * wdw_ref[k:k + 1, c0:c1]
            y_ref[base:base + CONV_CHUNK, c0:c1] = acc


def _layernorm(y, g, b):
    mu = jnp.mean(y, axis=-1, keepdims=True)
    d = y - mu
    var = jnp.mean(d * d, axis=-1, keepdims=True)
    return d * lax.rsqrt(var + EPS) * g + b


def _attend(q, k, v, xhead_dim):
    s = lax.dot_general(q.astype(BF16), k, (((1,), (1,)), ((), ())), preferred_element_type=F32)
    s = s * (xhead_dim ** -0.5)
    e = jnp.exp(s - jnp.max(s, axis=-1, keepdims=True))
    l = jnp.sum(e, axis=-1, keepdims=True)
    return jnp.dot(e.astype(BF16), v, preferred_element_type=F32) / l


def _prompt_mixer_kernel(z_ref, k_ref, v_ref, wpool_ref, pscale_ref, wdw_ref, bdw_ref, lng_ref, lnb_ref, wpw_ref,
                         o_ref, pstate_ref, cstate_ref, extp_ref, extc_ref, pooled_ref, y_ref,
                         *, tt, d_pool, d_conv, d_xattn):
    t_idx = pl.program_id(1)
    n_t = pl.num_programs(1)
    pg = d_pool // len(POOL_WINDOWS)
    xhead_dim = d_xattn // N_XHEADS
    off_gate_a = d_pool
    off_val = 2 * d_pool
    off_glu = off_val + d_conv
    off_gate_b = off_glu + d_conv
    off_q = off_gate_b + d_conv
    off_gate_c = off_q + d_xattn

    @pl.when(t_idx == 0)
    def _():
        extp_ref[:, 0:POOL_HALO, :] = jnp.zeros((d_pool // LANES, POOL_HALO, LANES), F32)
        extc_ref[:, 0:CONV_HALO, :] = jnp.zeros((d_conv // LANES, CONV_HALO, LANES), F32)

    for cb in range(d_pool // LANES):
        extp_ref[cb, POOL_HALO:POOL_HALO + tt, :] = z_ref[:, cb * LANES:(cb + 1) * LANES]
    _pool_tile(extp_ref, pooled_ref, tt, d_pool, t_idx)
    for gi in range(len(POOL_WINDOWS)):
        c0, c1 = gi * pg, (gi + 1) * pg
        mixed = jnp.dot(pooled_ref[:, c0:c1].astype(BF16), wpool_ref[gi], preferred_element_type=F32)
        mixed = mixed * pscale_ref[:, c0:c1]
        o_ref[:, c0:c1] = (mixed * z_ref[:, off_gate_a + c0:off_gate_a + c1]).astype(BF16)

    for cb in range(d_conv // LANES):
        c0, c1 = cb * LANES, (cb + 1) * LANES
        extc_ref[cb, CONV_HALO:CONV_HALO + tt, :] = z_ref[:, off_val + c0:off_val + c1] * z_ref[:, off_glu + c0:off_glu + c1]
    _dwconv_tile(extc_ref, wdw_ref, bdw_ref, y_ref, tt, range(d_conv // LANES))
    act = _silu(_layernorm(y_ref[...], lng_ref[...], lnb_ref[...]))
    o_b = jnp.dot(act.astype(BF16), wpw_ref[...], preferred_element_type=F32)
    o_ref[:, d_pool:d_pool + d_conv] = (o_b * z_ref[:, off_gate_b:off_gate_b + d_conv]).astype(BF16)

    for h in range(N_XHEADS):
        c0, c1 = h * xhead_dim, (h + 1) * xhead_dim
        o_c = _attend(z_ref[:, off_q + c0:off_q + c1], k_ref[0, :, c0:c1].astype(BF16),
                      v_ref[0, :, c0:c1].astype(BF16), xhead_dim)
        o_c = o_c * z_ref[:, off_gate_c + c0:off_gate_c + c1]
        o_ref[:, d_pool + d_conv + c0:d_pool + d_conv + c1] = o_c.astype(BF16)

    @pl.when(t_idx == n_t - 1)
    def _():
        for cb in range(d_pool // LANES):
            pstate_ref[0, :, cb * LANES:(cb + 1) * LANES] = extp_ref[cb, tt + 1:tt + POOL_HALO, :]
        for cb in range(d_conv // LANES):
            cstate_ref[0, :, cb * LANES:(cb + 1) * LANES] = extc_ref[cb, tt + 2:tt + CONV_HALO, :]

    extp_ref[:, 0:POOL_HALO, :] = extp_ref[:, tt:tt + POOL_HALO, :]
    extc_ref[:, 0:CONV_HALO, :] = extc_ref[:, tt:tt + CONV_HALO, :]


def _prompt_mixer(z, k, v, w_pool, pool_scale, w_dw, b_dw, ln_g, ln_b, w_pw, *, batch, seq, tt, d_model):
    d_in = z.shape[1]
    d_pool = w_pool.shape[0] * w_pool.shape[1]
    d_conv = w_pw.shape[0]
    d_xattn = k.shape[2]
    n_mem = k.shape[1]
    n_t = seq // tt
    assert seq % tt == 0 and tt % CONV_CHUNK == 0 and tt % POOL_CHUNK == 0
    kernel = functools.partial(_prompt_mixer_kernel, tt=tt, d_pool=d_pool, d_conv=d_conv, d_xattn=d_xattn)
    const2 = lambda b, t: (0, 0)
    return pl.pallas_call(
        kernel,
        grid=(batch, n_t),
        in_specs=[
            pl.BlockSpec((tt, d_in), lambda b, t: (b * n_t + t, 0)),
            pl.BlockSpec((1, n_mem, d_xattn), lambda b, t: (b, 0, 0)),
            pl.BlockSpec((1, n_mem, d_xattn), lambda b, t: (b, 0, 0)),
            pl.BlockSpec(w_pool.shape, lambda b, t: (0, 0, 0)),
            pl.BlockSpec((1, d_pool), const2),
            pl.BlockSpec(w_dw.shape, const2),
            pl.BlockSpec((1, d_conv), const2),
            pl.BlockSpec((1, d_conv), const2),
            pl.BlockSpec((1, d_conv), const2),
            pl.BlockSpec(w_pw.shape, const2),
        ],
        out_specs=[
            pl.BlockSpec((tt, d_model), lambda b, t: (b * n_t + t, 0)),
            pl.BlockSpec((1, POOL_HALO - 1, d_pool), lambda b, t: (b, 0, 0)),
            pl.BlockSpec((1, CONV_HALO - 2, d_conv), lambda b, t: (b, 0, 0)),
        ],
        out_shape=[
            jax.ShapeDtypeStruct((batch * seq, d_model), BF16),
            jax.ShapeDtypeStruct((batch, POOL_HALO - 1, d_pool), F32),
            jax.ShapeDtypeStruct((batch, CONV_HALO - 2, d_conv), F32),
        ],
        scratch_shapes=[
            pltpu.VMEM((d_pool // LANES, POOL_HALO + tt, LANES), F32),
            pltpu.VMEM((d_conv // LANES, CONV_HALO + tt, LANES), F32),
            pltpu.VMEM((tt, d_pool), F32),
            pltpu.VMEM((tt, d_conv), F32),
        ],
        compiler_params=_compiler_params(("arbitrary", "arbitrary")),
        name="prompt_mixer",
    )(z, k, v, w_pool, pool_scale, w_dw, b_dw, ln_g, ln_b, w_pw)


def _sample_mixer_kernel(z_ref, pst_ref, cst_ref, wpool_ref, pscale_ref, wdw_ref, bdw_ref, lng_ref, lnb_ref, wpw_ref,
                         o_ref, pnew_ref, cnew_ref, *, dec_seq, d_pool, d_conv):
    pg = d_pool // len(POOL_WINDOWS)
    n_pst = pst_ref.shape[0]
    n_cst = cst_ref.shape[0]
    gb = z_ref.shape[1]
    off_gate_a = d_pool
    off_val = 2 * d_pool
    off_glu = off_val + d_conv
    off_gate_b = off_glu + d_conv

    def pool_row(r, c0, c1):
        if r < n_pst:
            return pst_ref[r, :, c0:c1]
        return z_ref[r - n_pst, :, c0:c1]

    for gi, w in enumerate(POOL_WINDOWS):
        c0, c1 = gi * pg, (gi + 1) * pg
        pooled = []
        for t in range(dec_seq):
            s = pool_row(n_pst + t, c0, c1)
            cur = s
            for k in range(1, w):
                s = s + pool_row(n_pst + t - k, c0, c1)
            cnt = float(min(w, PAST_LEN + t + 1))
            pooled.append((s / cnt - cur).astype(BF16))
        mixed = jnp.dot(jnp.concatenate(pooled, axis=0), wpool_ref[gi], preferred_element_type=F32)
        mixed = mixed * pscale_ref[:, c0:c1]
        for t in range(dec_seq):
            gate = z_ref[t, :, off_gate_a + c0:off_gate_a + c1]
            o_ref[t, :, c0:c1] = (mixed[t * gb:(t + 1) * gb] * gate).astype(BF16)
    for r in range(n_pst):
        pnew_ref[r] = pool_row(r + dec_seq, 0, d_pool)

    cw = 2 * LANES
    ys = [[] for _ in range(dec_seq)]
    for cb in range(d_conv // cw):
        c0, c1 = cb * cw, (cb + 1) * cw
        a_new = [z_ref[t, :, off_val + c0:off_val + c1] * z_ref[t, :, off_glu + c0:off_glu + c1]
                 for t in range(dec_seq)]

        def conv_row(r, c0=c0, c1=c1, a_new=a_new):
            if r < n_cst:
                return cst_ref[r, :, c0:c1]
            return a_new[r - n_cst]

        acc = [jnp.broadcast_to(bdw_ref[:, c0:c1], (gb, cw)) for _ in range(dec_seq)]
        for r in range(n_cst + dec_seq):
            x_r = conv_row(r)
            for t in range(dec_seq):
                k = r - t
                if 0 <= k < CONV_WIDTH:
                    acc[t] = acc[t] + x_r * wdw_ref[k:k + 1, c0:c1]
            if r >= dec_seq:
                cnew_ref[r - dec_seq, :, c0:c1] = x_r
        for t in range(dec_seq):
            ys[t].append(acc[t])
    y = jnp.concatenate([jnp.concatenate(ys[t], axis=1) for t in range(dec_seq)], axis=0)
    act = _silu(_layernorm(y, lng_ref[...], lnb_ref[...]))
    o_b = jnp.dot(act.astype(BF16), wpw_ref[...], preferred_element_type=F32)
    for t in range(dec_seq):
        gate = z_ref[t, :, off_gate_b:off_gate_b + d_conv]
        o_ref[t, :, d_pool:d_pool + d_conv] = (o_b[t * gb:(t + 1) * gb] * gate).astype(BF16)


def _sample_mixer(z3, pstate, cstate, w_pool, pool_scale, w_dw, b_dw, ln_g, ln_b, w_pw, *, gb):
    dec_seq, dec_batch, d_in = z3.shape
    d_pool = pstate.shape[2]
    d_conv = cstate.shape[2]
    assert dec_batch % gb == 0
    kernel = functools.partial(_sample_mixer_kernel, dec_seq=dec_seq, d_pool=d_pool, d_conv=d_conv)
    const2 = lambda i: (0, 0)
    seq_block = lambda a: pl.BlockSpec((a.shape[0], gb, a.shape[2]), lambda i: (0, i, 0))
    return pl.pallas_call(
        kernel,
        grid=(dec_batch // gb,),
        in_specs=[
            seq_block(z3),
            seq_block(pstate),
            seq_block(cstate),
            pl.BlockSpec(w_pool.shape, lambda i: (0, 0, 0)),
            pl.BlockSpec((1, d_pool), const2),
            pl.BlockSpec(w_dw.shape, const2),
            pl.BlockSpec((1, d_conv), const2),
            pl.BlockSpec((1, d_conv), const2),
            pl.BlockSpec((1, d_conv), const2),
            pl.BlockSpec(w_pw.shape, const2),
        ],
        out_specs=[
            pl.BlockSpec((dec_seq, gb, d_pool + d_conv), lambda i: (0, i, 0)),
            seq_block(pstate),
            seq_block(cstate),
        ],
        out_shape=[
            jax.ShapeDtypeStruct((dec_seq, dec_batch, d_pool + d_conv), BF16),
            jax.ShapeDtypeStruct(pstate.shape, F32),
            jax.ShapeDtypeStruct(cstate.shape, F32),
        ],
        compiler_params=_compiler_params(("arbitrary",)),
        name="sample_mixer",
    )(z3, pstate, cstate, w_pool, pool_scale, w_dw, b_dw, ln_g, ln_b, w_pw)


def _sample_attn_kernel(q_ref, gate_ref, k_ref, v_ref, o_ref, *, scale):
    gb, n_rows, _ = k_ref.shape
    n_q = q_ref.shape[1]
    half_rows = n_q // 2
    col = lax.broadcasted_iota(jnp.int32, (n_q, n_rows), 1)
    row = lax.broadcasted_iota(jnp.int32, (n_q, n_rows), 0)
    match = (((col >> 2) & 1) == (row >> 4)) & ((col & 3) == ((row >> 2) & 3))
    valid = match[0:half_rows]
    seqs = range(gb)
    p = [lax.dot_general(q_ref[b].astype(BF16), k_ref[b].astype(BF16), (((1,), (1,)), ((), ())),
                         preferred_element_type=F32) for b in seqs]
    p = [jnp.where(match, p[b], 0.0) for b in seqs]
    s = [p[b][0:half_rows] + pltpu.roll(p[b][half_rows:n_q], n_rows - 4, axis=1) for b in seqs]
    s = [jnp.where(valid, s[b] * scale, -1e30) for b in seqs]
    m = [jnp.max(s[b], axis=-1, keepdims=True) for b in seqs]
    e = [jnp.where(valid, jnp.exp(s[b] - m[b]), 0.0) for b in seqs]
    l = [jnp.sum(e[b], axis=-1, keepdims=True) for b in seqs]
    e2 = [jnp.concatenate([e[b], pltpu.roll(e[b], 4, axis=1)], axis=0).astype(BF16) for b in seqs]
    o = [jnp.dot(e2[b], v_ref[b].astype(BF16), preferred_element_type=F32) for b in seqs]
    for b in seqs:
        o_ref[b] = o[b] / jnp.concatenate([l[b], l[b]], axis=0) * gate_ref[b]


def _sample_attn(qm, gm, cache_k, cache_v, *, gb, xhead_dim):
    nb, n_q, lanes = qm.shape
    n_rows = cache_k.shape[1]
    assert n_q == 2 * N_XHEADS * 4 and N_XHEADS == 4 and lanes == LANES
    kernel = functools.partial(_sample_attn_kernel, scale=xhead_dim ** -0.5)
    small = pl.BlockSpec((gb, n_q, lanes), lambda i: (i, 0, 0))
    big = pl.BlockSpec((gb, n_rows, lanes), lambda i: (i, 0, 0))
    return pl.pallas_call(
        kernel,
        grid=(nb // gb,),
        in_specs=[small, small, big, big],
        out_specs=small,
        out_shape=jax.ShapeDtypeStruct((nb, n_q, lanes), F32),
        compiler_params=_compiler_params(("arbitrary",)),
        name="sample_attn",
    )(qm, gm, cache_k, cache_v)


def _cache_rows(c):
    nb, n_mem, n_heads, e = c.shape
    assert n_heads == N_XHEADS and e == 2 * LANES
    return c.reshape(nb, n_mem, n_heads, 2, LANES).transpose(0, 1, 3, 2, 4).reshape(nb, n_mem * 2 * n_heads, LANES)


def _query_rows(q):
    t, nb, _ = q.shape
    return q.reshape(t, nb, N_XHEADS, 2, LANES).transpose(1, 3, 2, 0, 4).reshape(nb, 2 * N_XHEADS * t, LANES)


def _query_rows_inv(o, t):
    nb = o.shape[0]
    return o.reshape(nb, 2, N_XHEADS, t, LANES).transpose(3, 0, 2, 1, 4).reshape(t, nb, N_XHEADS * 2 * LANES)


def _out_proj_kernel(o_ref, w_ref, x_ref, g_ref, y_hbm, acc_ref, ssq_ref, scale_ref, stage_ref, sem, stage_sem,
                     *, n_i, n_j, tm, tn, d_model):
    i = pl.program_id(0)
    j = pl.program_id(1)
    slot = j % 2

    def dst(jj, row_tile):
        return y_hbm.at[pl.ds(pl.multiple_of(row_tile * tm, tm), tm), jj * tn:(jj + 1) * tn]

    def stage_copy(jj, row_tile):
        return pltpu.make_async_copy(stage_ref.at[jj % 2], dst(jj, row_tile), stage_sem.at[jj % 2])

    def slab_copy(jj, row_tile):
        return pltpu.make_async_copy(acc_ref.at[jj], dst(jj, row_tile), sem.at[jj])

    @pl.when((i == 0) & (j == 0))
    def _():
        acc_ref[...] = jnp.zeros(acc_ref.shape, F32)
        scale_ref[...] = jnp.zeros(scale_ref.shape, F32)

    for jj in range(n_j):
        if jj >= 2:

            @pl.when((j == jj) & (i > 0))
            def _(jj=jj):
                stage_copy(jj - 2, i - 1).wait()
        else:

            @pl.when((j == jj) & (i > 1))
            def _(jj=jj):
                stage_copy(jj + n_j - 2, i - 2).wait()

    stage_ref[slot] = acc_ref[j] * scale_ref[...] * g_ref[j]
    part = x_ref[...] + jnp.dot(o_ref[...], w_ref[...], preferred_element_type=F32)
    acc_ref[j] = part
    row_ssq = jnp.sum(part * part, axis=-1, keepdims=True)

    for jj in range(n_j):

        @pl.when((j == jj) & (i > 0))
        def _(jj=jj):
            stage_copy(jj, i - 1).start()

    @pl.when(j == 0)
    def _():
        ssq_ref[...] = row_ssq

    @pl.when(j > 0)
    def _():
        ssq_ref[...] += row_ssq

    @pl.when(j == n_j - 1)
    def _():
        scale_ref[...] = lax.rsqrt(ssq_ref[...] * (1.0 / d_model) + EPS)

    @pl.when((i == n_i - 1) & (j == n_j - 1))
    def _():
        if n_i > 1:
            stage_copy(n_j - 2, i - 1).wait()
            stage_copy(n_j - 1, i - 1).wait()
        for jj in range(n_j):
            acc_ref[jj] = acc_ref[jj] * scale_ref[...] * g_ref[jj]
            slab_copy(jj, i).start()
        for jj in range(n_j):
            slab_copy(jj, i).wait()


def _out_proj(o, w, x, g, *, tile):
    tm, tn = tile
    m, k = o.shape
    n = w.shape[1]
    n_j = n // tn
    assert m % tm == 0 and n % tn == 0 and n_j % 2 == 0
    n_i = m // tm
    kernel = functools.partial(_out_proj_kernel, n_i=n_i, n_j=n_j, tm=tm, tn=tn, d_model=n)
    g_tiles = g.reshape(n_j, 1, tn)
    return pl.pallas_call(
        kernel,
        grid=(n_i, n_j),
        in_specs=[
            pl.BlockSpec((tm, k), lambda i, j: (i, 0)),
            pl.BlockSpec((k, tn), lambda i, j: (0, j)),
            pl.BlockSpec((tm, tn), lambda i, j: (i, j)),
            pl.BlockSpec((n_j, 1, tn), lambda i, j: (0, 0, 0)),
        ],
        out_specs=pl.BlockSpec(memory_space=pl.ANY),
        out_shape=jax.ShapeDtypeStruct((m, n), F32),
        scratch_shapes=[
            pltpu.VMEM((n_j, tm, tn), F32),
            pltpu.VMEM((tm, 1), F32),
            pltpu.VMEM((tm, 1), F32),
            pltpu.VMEM((2, tm, tn), F32),
            pltpu.SemaphoreType.DMA((n_j,)),
            pltpu.SemaphoreType.DMA((2,)),
        ],
        compiler_params=_compiler_params(("arbitrary", "arbitrary")),
        name="out_proj",
    )(o, w, x, g_tiles)


def kernel(x_prompt, mem_prompt, x_sample, cache_mem_k, cache_mem_v, state_pool, state_conv, norm_g, mem_norm_g, w_in,
           w_mem_k, w_mem_v, w_pool, pool_scale, w_dw, b_dw, conv_ln_g, conv_ln_b, w_pw, w_out, final_norm_g):
    depth = w_in.shape[0]
    assert depth == 1, "single-layer step"
    batch, seq, d_model = x_prompt.shape
    dec_batch, dec_seq, _ = x_sample.shape
    n_mem = mem_prompt.shape[1]
    d_xattn = w_mem_k.shape[2]
    d_pool = pool_scale.shape[1]
    d_conv = w_pw.shape[1]
    xhead_dim = d_xattn // N_XHEADS
    off_q = 2 * d_pool + 3 * d_conv
    l = 0

    w_pool_b = w_pool[l].astype(BF16)
    w_pw_b = w_pw[l].astype(BF16)
    g_in = norm_g[l][None, :]
    g_mem = mem_norm_g[l][None, :]
    g_fin = final_norm_g[None, :]
    pscale = pool_scale[l][None, :]
    bdw = b_dw[l][None, :]
    lng = conv_ln_g[l][None, :]
    lnb = conv_ln_b[l][None, :]
    mixer_w = (w_pool_b, pscale, w_dw[l], bdw, lng, lnb, w_pw_b)
    z_segments = ((d_pool, ACT_NONE), (d_pool, ACT_SILU), (d_conv, ACT_NONE), (d_conv, ACT_SIGMOID),
                  (d_conv, ACT_SILU), (d_xattn, ACT_NONE), (d_xattn, ACT_SILU))

    xp = x_prompt.reshape(batch * seq, d_model)
    mem = mem_prompt.reshape(batch * n_mem, d_model)
    k_p, v_p = _norm_matmul_pair(mem, g_mem, w_mem_k[l], w_mem_v[l], tile=MEM_PROJ_TILE)
    k_p = k_p.reshape(batch, n_mem, d_xattn)
    v_p = v_p.reshape(batch, n_mem, d_xattn)
    xs = x_sample.transpose(1, 0, 2).reshape(dec_seq * dec_batch, d_model)
    z_p, z_s, w_out_b = _stream_norm_matmul(
        xp, xs, g_in, w_in[l], w_out[l], tile=PROMPT_PROJ_TILE, n_chunk=PROMPT_PROJ_NORM_CHUNKS,
        cast_rows=W_OUT_CAST_ROWS, act_ranges=_act_tile_ranges(z_segments, PROMPT_PROJ_TILE[1]))
    o_p, pool_p, conv_p = _prompt_mixer(z_p, k_p, v_p, *mixer_w,
                                        batch=batch, seq=seq, tt=PROMPT_MIXER_ROWS, d_model=d_model)
    y_p = _out_proj(o_p, w_out_b, xp, g_fin, tile=OUT_PROJ_TILE).reshape(batch, seq, d_model)

    z_s = z_s.reshape(dec_seq, dec_batch, -1)
    o_ab, pool_s, conv_s = _sample_mixer(z_s, state_pool[l].transpose(1, 0, 2), state_conv[l].transpose(1, 0, 2),
                                         *mixer_w, gb=SAMPLE_MIXER_SEQS)
    o_c = _sample_attn(_query_rows(z_s[:, :, off_q:off_q + d_xattn]), _query_rows(z_s[:, :, off_q + d_xattn:]),
                       _cache_rows(cache_mem_k[l]), _cache_rows(cache_mem_v[l]),
                       gb=SAMPLE_ATTN_SEQS, xhead_dim=xhead_dim)
    o_c = _query_rows_inv(o_c, dec_seq)
    o_s = jnp.concatenate([o_ab, o_c.astype(BF16)], axis=-1).reshape(dec_seq * dec_batch, d_model)
    y_s = _out_proj(o_s, w_out_b, xs, g_fin, tile=SAMPLE_OUT_PROJ_TILE)
    y_s = y_s.reshape(dec_seq, dec_batch, d_model).transpose(1, 0, 2)

    mem_shape = (depth, batch, n_mem, N_XHEADS, xhead_dim)
    return (y_p, y_s, k_p.reshape(mem_shape), v_p.reshape(mem_shape), pool_p[None], conv_p[None],
            pool_s.transpose(1, 0, 2)[None], conv_s.transpose(1, 0, 2)[None])
```

```python
import functools

import jax
import jax.numpy as jnp
from jax import lax
from jax.experimental import pallas as pl
from jax.experimental.pallas import tpu as pltpu

EPS = 1e-6
POOL_WINDOWS = (2, 4, 8, 16)
CONV_WIDTH = 31
N_XHEADS = 4
PAST_LEN = 16384

LANES = 128
VMEM_LIMIT_BYTES = 56 * 1024 * 1024

PROMPT_PROJ_TILE = (1024, 512)
PROMPT_PROJ_NORM_CHUNKS = 16
WEIGHT_CAST_ROWS = 64
MEM_PROJ_TILE = (512, 256)
OUT_PROJ_TILE = (1024, 512)
SAMPLE_OUT_PROJ_TILE = (512, 512)
PROMPT_MIXER_ROWS = 256
POOL_CHUNK = 32
CONV_CHUNK = 64
SAMPLE_MIXER_SEQS = 16
SAMPLE_ATTN_SEQS = 8

BF16 = jnp.bfloat16
F32 = jnp.float32

ACT_NONE, ACT_SILU, ACT_SIGMOID = 0, 1, 2


def _sigmoid(x):
    return 0.5 * jnp.tanh(0.5 * x) + 0.5


def _silu(x):
    return x * _sigmoid(x)


def _activate(r, act):
    if act == ACT_SILU:
        return _silu(r)
    if act == ACT_SIGMOID:
        return _sigmoid(r)
    return r


def _rms_scale(x):
    return lax.rsqrt(jnp.mean(x * x, axis=-1, keepdims=True) + EPS)


def _compiler_params(semantics):
    return pltpu.CompilerParams(dimension_semantics=semantics, vmem_limit_bytes=VMEM_LIMIT_BYTES)


def _act_tile_ranges(segments, tn):
    ranges, off = [], 0
    for width, act in segments:
        assert off % tn == 0 and width % tn == 0, "segments must be whole column tiles"
        if act != ACT_NONE:
            ranges.append((off // tn, (off + width) // tn, act))
        off += width
    return tuple(ranges)


def _tile_act(j, act_ranges):
    act = jnp.int32(ACT_NONE)
    for lo, hi, a in act_ranges:
        act = jnp.where((j >= lo) & (j < hi), a, act)
    return act


def _acts_used(act_ranges):
    return sorted({ACT_NONE} | {a for _, _, a in act_ranges})


def _norm_matmul_pair_kernel(x_ref, g_ref, wa_ref, wb_ref, oa_ref, ob_ref, h_ref):
    @pl.when(pl.program_id(1) == 0)
    def _():
        x = x_ref[...]
        h_ref[...] = (x * _rms_scale(x) * g_ref[...]).astype(BF16)

    oa_ref[...] = jnp.dot(h_ref[...], wa_ref[...].astype(BF16), preferred_element_type=F32)
    ob_ref[...] = jnp.dot(h_ref[...], wb_ref[...].astype(BF16), preferred_element_type=F32)


def _stream_norm_matmul_kernel(x_ref, xs_ref, g_ref, w_ref, *refs, n_j, n_chunk, n_chunk_s, chunk, n_cast,
                               act_ranges):
    cast_src_refs = refs[:n_cast]
    o_ref, os_ref = refs[n_cast:n_cast + 2]
    cast_dst_refs = refs[n_cast + 2:2 * n_cast + 2]
    h_ref, hs_ref = refs[2 * n_cast + 2:]
    s = pl.program_id(0)
    n_pro = n_chunk_s + n_chunk
    step = jnp.maximum(s - n_pro, 0)
    i = step // n_j
    j = step % n_j

    def normalised(src_ref):
        x = src_ref[...]
        return (x * _rms_scale(x) * g_ref[...]).astype(BF16)

    def chunk_rows(c):
        return pl.ds(pl.multiple_of(c * chunk, chunk), chunk)

    @pl.when(s < n_chunk_s)
    def _():
        hs_ref[chunk_rows(s), :] = normalised(xs_ref)

    @pl.when((s >= n_chunk_s) & (s < n_pro))
    def _():
        h_ref[0, chunk_rows(s - n_chunk_s), :] = normalised(x_ref)

    tile_act = _tile_act(j, act_ranges)
    for act in _acts_used(act_ranges):

        @pl.when((s >= n_pro) & (tile_act == act))
        def _(act=act):
            h_ref[(i + 1) % 2, chunk_rows(jnp.clip(j - 1, 0, n_chunk - 1)), :] = normalised(x_ref)
            for src_ref, dst_ref in zip(cast_src_refs, cast_dst_refs):
                dst_ref[...] = src_ref[...].astype(BF16)
            w = w_ref[...].astype(BF16)
            o_ref[...] = _activate(jnp.dot(h_ref[i % 2], w, preferred_element_type=F32), act)

            @pl.when(i == 0)
            def _():
                os_ref[...] = _activate(jnp.dot(hs_ref[...], w, preferred_element_type=F32), act)


def _stream_norm_matmul(x, xs, g, w, cast_srcs, *, tile, n_chunk, cast_rows, act_ranges=()):
    tm, tn = tile
    m, k = x.shape
    ms = xs.shape[0]
    n = w.shape[1]
    n_i, n_j = m // tm, n // tn
    chunk = tm // n_chunk
    n_chunk_s = ms // chunk
    n_pro = n_chunk_s + n_chunk
    assert m % tm == 0 and n % tn == 0 and tm % n_chunk == 0 and n_chunk < n_j
    assert ms % chunk == 0 and ms <= tm
    for src in cast_srcs:
        assert src.ndim == 2 and src.shape[0] % cast_rows == 0 and src.shape[0] // cast_rows <= n_i * n_j

    def steps(s):
        step = jnp.maximum(s - n_pro, 0)
        return step // n_j, step % n_j

    def x_index(s):
        i, j = steps(s)
        nxt = jnp.minimum((i + 1) * n_chunk + jnp.clip(j - 1, 0, n_chunk - 1), n_i * n_chunk - 1)
        return jnp.where(s < n_pro, jnp.maximum(s - n_chunk_s, 0), nxt), 0

    def os_index(s):
        i, j = steps(s)
        return 0, jnp.where(i == 0, j, n_j - 1)

    def cast_spec(src):
        n_blocks = src.shape[0] // cast_rows
        return pl.BlockSpec((cast_rows, src.shape[1]),
                            lambda s: (jnp.minimum(jnp.maximum(s - n_pro, 0), n_blocks - 1), 0))

    cast_specs = [cast_spec(src) for src in cast_srcs]
    kernel = functools.partial(_stream_norm_matmul_kernel, n_j=n_j, n_chunk=n_chunk, n_chunk_s=n_chunk_s,
                               chunk=chunk, n_cast=len(cast_srcs), act_ranges=act_ranges)
    z, zs, *casts = pl.pallas_call(
        kernel,
        grid=(n_pro + n_i * n_j,),
        in_specs=[
            pl.BlockSpec((chunk, k), x_index),
            pl.BlockSpec((chunk, k), lambda s: (jnp.minimum(s, n_chunk_s - 1), 0)),
            pl.BlockSpec((1, k), lambda s: (0, 0)),
            pl.BlockSpec((k, tn), lambda s: (0, steps(s)[1])),
            *cast_specs,
        ],
        out_specs=[
            pl.BlockSpec((tm, tn), lambda s: steps(s)),
            pl.BlockSpec((ms, tn), os_index),
            *cast_specs,
        ],
        out_shape=[
            jax.ShapeDtypeStruct((m, n), F32),
            jax.ShapeDtypeStruct((ms, n), F32),
            *[jax.ShapeDtypeStruct(src.shape, BF16) for src in cast_srcs],
        ],
        scratch_shapes=[pltpu.VMEM((2, tm, k), BF16), pltpu.VMEM((ms, k), BF16)],
        compiler_params=_compiler_params(("arbitrary",)),
        name="stream_norm_matmul",
    )(x, xs, g, w, *cast_srcs)
    return z, zs, casts


def _norm_matmul_pair(x, g, w_a, w_b, *, tile):
    tm, tn = tile
    m, k = x.shape
    n = w_a.shape[1]
    assert m % tm == 0 and n % tn == 0 and w_b.shape == w_a.shape
    w_spec = pl.BlockSpec((k, tn), lambda i, j: (0, j))
    o_spec = pl.BlockSpec((tm, tn), lambda i, j: (i, j))
    return pl.pallas_call(
        _norm_matmul_pair_kernel,
        grid=(m // tm, n // tn),
        in_specs=[pl.BlockSpec((tm, k), lambda i, j: (i, 0)), pl.BlockSpec((1, k), lambda i, j: (0, 0)), w_spec, w_spec],
        out_specs=[o_spec, o_spec],
        out_shape=[jax.ShapeDtypeStruct((m, n), F32)] * 2,
        scratch_shapes=[pltpu.VMEM((tm, k), BF16)],
        compiler_params=_compiler_params(("arbitrary", "arbitrary")),
        name="norm_matmul_pair",
    )(x, g, w_a, w_b)


POOL_HALO = 16
CONV_HALO = 32


def _pool_tile(extp_ref, pooled_ref, tt, d_pool, t_idx):
    pg = d_pool // len(POOL_WINDOWS)
    row = lax.broadcasted_iota(jnp.int32, (POOL_CHUNK, LANES), 0)

    for base in range(0, tt, POOL_CHUNK):
        n_prev = t_idx * tt + base + row + 1
        for gi, w in enumerate(POOL_WINDOWS):
            inv_cnt = 1.0 / jnp.minimum(w, n_prev).astype(F32)
            for cb in range(gi * pg // LANES, (gi + 1) * pg // LANES):
                cur = extp_ref[cb, base + POOL_HALO:base + POOL_HALO + POOL_CHUNK, :]
                s = cur
                for k in range(1, w):
                    s = s + extp_ref[cb, base + POOL_HALO - k:base + POOL_HALO - k + POOL_CHUNK, :]
                pooled_ref[base:base + POOL_CHUNK, cb * LANES:(cb + 1) * LANES] = s * inv_cnt - cur


def _dwconv_tile(extc_ref, wdw_ref, bdw_ref, y_ref, tt, col_blocks):
    first = CONV_HALO - (CONV_WIDTH - 1)
    for cb in col_blocks:
        c0, c1 = cb * LANES, (cb + 1) * LANES
        for base in range(0, tt, CONV_CHUNK):
            acc = jnp.broadcast_to(bdw_ref[:, c0:c1], (CONV_CHUNK, LANES))
            for k in range(CONV_WIDTH):
                acc = acc + extc_ref[cb, base + first + k:base + first + k + CONV_CHUNK, :] * wdw_ref[k:k + 1, c0:c1]
            y_ref[base:base + CONV_CHUNK, c0:c1] = acc


def _layernorm(y, g, b):
    mu = jnp.mean(y, axis=-1, keepdims=True)
    d = y - mu
    var = jnp.mean(d * d, axis=-1, keepdims=True)
    return d * lax.rsqrt(var + EPS) * g + b


def _attend(q, k, v, xhead_dim):
    s = lax.dot_general(q.astype(BF16), k, (((1,), (1,)), ((), ())), preferred_element_type=F32)
    s = s * (xhead_dim ** -0.5)
    e = jnp.exp(s - jnp.max(s, axis=-1, keepdims=True))
    l = jnp.sum(e, axis=-1, keepdims=True)
    return jnp.dot(e.astype(BF16), v, preferred_element_type=F32) / l


def _prompt_mixer_kernel(z_ref, k_ref, v_ref, wpool_ref, pscale_ref, wdw_ref, bdw_ref, lng_ref, lnb_ref, wpw_ref,
                         o_ref, pstate_ref, cstate_ref, extp_ref, extc_ref, pooled_ref, y_ref,
                         *, tt, d_pool, d_conv, d_xattn):
    t_idx = pl.program_id(1)
    n_t = pl.num_programs(1)
    pg = d_pool // len(POOL_WINDOWS)
    xhead_dim = d_xattn // N_XHEADS
    off_gate_a = d_pool
    off_val = 2 * d_pool
    off_glu = off_val + d_conv
    off_gate_b = off_glu + d_conv
    off_q = off_gate_b + d_conv
    off_gate_c = off_q + d_xattn

    @pl.when(t_idx == 0)
    def _():
        extp_ref[:, 0:POOL_HALO, :] = jnp.zeros((d_pool // LANES, POOL_HALO, LANES), F32)
        extc_ref[:, 0:CONV_HALO, :] = jnp.zeros((d_conv // LANES, CONV_HALO, LANES), F32)

    for cb in range(d_pool // LANES):
        extp_ref[cb, POOL_HALO:POOL_HALO + tt, :] = z_ref[:, cb * LANES:(cb + 1) * LANES]
    _pool_tile(extp_ref, pooled_ref, tt, d_pool, t_idx)
    for gi in range(len(POOL_WINDOWS)):
        c0, c1 = gi * pg, (gi + 1) * pg
        mixed = jnp.dot(pooled_ref[:, c0:c1].astype(BF16), wpool_ref[gi], preferred_element_type=F32)
        mixed = mixed * pscale_ref[:, c0:c1]
        o_ref[:, c0:c1] = (mixed * z_ref[:, off_gate_a + c0:off_gate_a + c1]).astype(BF16)

    for cb in range(d_conv // LANES):
        c0, c1 = cb * LANES, (cb + 1) * LANES
        extc_ref[cb, CONV_HALO:CONV_HALO + tt, :] = z_ref[:, off_val + c0:off_val + c1] * z_ref[:, off_glu + c0:off_glu + c1]
    _dwconv_tile(extc_ref, wdw_ref, bdw_ref, y_ref, tt, range(d_conv // LANES))
    act = _silu(_layernorm(y_ref[...], lng_ref[...], lnb_ref[...]))
    o_b = jnp.dot(act.astype(BF16), wpw_ref[...], preferred_element_type=F32)
    o_ref[:, d_pool:d_pool + d_conv] = (o_b * z_ref[:, off_gate_b:off_gate_b + d_conv]).astype(BF16)

    for h in range(N_XHEADS):
        c0, c1 = h * xhead_dim, (h + 1) * xhead_dim
        o_c = _attend(z_ref[:, off_q + c0:off_q + c1], k_ref[0, :, c0:c1].astype(BF16),
                      v_ref[0, :, c0:c1].astype(BF16), xhead_dim)
        o_c = o_c * z_ref[:, off_gate_c + c0:off_gate_c + c1]
        o_ref[:, d_pool + d_conv + c0:d_pool + d_conv + c1] = o_c.astype(BF16)

    @pl.when(t_idx == n_t - 1)
    def _():
        for cb in range(d_pool // LANES):
            pstate_ref[0, :, cb * LANES:(cb + 1) * LANES] = extp_ref[cb, tt + 1:tt + POOL_HALO, :]
        for cb in range(d_conv // LANES):
            cstate_ref[0, :, cb * LANES:(cb + 1) * LANES] = extc_ref[cb, tt + 2:tt + CONV_HALO, :]

    extp_ref[:, 0:POOL_HALO, :] = extp_ref[:, tt:tt + POOL_HALO, :]
    extc_ref[:, 0:CONV_HALO, :] = extc_ref[:, tt:tt + CONV_HALO, :]


def _prompt_mixer(z, k, v, w_pool, pool_scale, w_dw, b_dw, ln_g, ln_b, w_pw, *, batch, seq, tt, d_model):
    d_in = z.shape[1]
    d_pool = w_pool.shape[0] * w_pool.shape[1]
    d_conv = w_pw.shape[0]
    d_xattn = k.shape[2]
    n_mem = k.shape[1]
    n_t = seq // tt
    assert seq % tt == 0 and tt % CONV_CHUNK == 0 and tt % POOL_CHUNK == 0
    kernel = functools.partial(_prompt_mixer_kernel, tt=tt, d_pool=d_pool, d_conv=d_conv, d_xattn=d_xattn)
    const2 = lambda b, t: (0, 0)
    return pl.pallas_call(
        kernel,
        grid=(batch, n_t),
        in_specs=[
            pl.BlockSpec((tt, d_in), lambda b, t: (b * n_t + t, 0)),
            pl.BlockSpec((1, n_mem, d_xattn), lambda b, t: (b, 0, 0)),
            pl.BlockSpec((1, n_mem, d_xattn), lambda b, t: (b, 0, 0)),
            pl.BlockSpec(w_pool.shape, lambda b, t: (0, 0, 0)),
            pl.BlockSpec((1, d_pool), const2),
            pl.BlockSpec(w_dw.shape, const2),
            pl.BlockSpec((1, d_conv), const2),
            pl.BlockSpec((1, d_conv), const2),
            pl.BlockSpec((1, d_conv), const2),
            pl.BlockSpec(w_pw.shape, const2),
        ],
        out_specs=[
            pl.BlockSpec((tt, d_model), lambda b, t: (b * n_t + t, 0)),
            pl.BlockSpec((1, POOL_HALO - 1, d_pool), lambda b, t: (b, 0, 0)),
            pl.BlockSpec((1, CONV_HALO - 2, d_conv), lambda b, t: (b, 0, 0)),
        ],
        out_shape=[
            jax.ShapeDtypeStruct((batch * seq, d_model), BF16),
            jax.ShapeDtypeStruct((batch, POOL_HALO - 1, d_pool), F32),
            jax.ShapeDtypeStruct((batch, CONV_HALO - 2, d_conv), F32),
        ],
        scratch_shapes=[
            pltpu.VMEM((d_pool // LANES, POOL_HALO + tt, LANES), F32),
            pltpu.VMEM((d_conv // LANES, CONV_HALO + tt, LANES), F32),
            pltpu.VMEM((tt, d_pool), F32),
            pltpu.VMEM((tt, d_conv), F32),
        ],
        compiler_params=_compiler_params(("arbitrary", "arbitrary")),
        name="prompt_mixer",
    )(z, k, v, w_pool, pool_scale, w_dw, b_dw, ln_g, ln_b, w_pw)


def _sample_mixer_kernel(z_ref, pst_ref, cst_ref, wpool_ref, pscale_ref, wdw_ref, bdw_ref, lng_ref, lnb_ref, wpw_ref,
                         o_ref, pnew_ref, cnew_ref, *, dec_seq, d_pool, d_conv):
    pg = d_pool // len(POOL_WINDOWS)
    n_pst = pst_ref.shape[0]
    n_cst = cst_ref.shape[0]
    gb = z_ref.shape[1]
    off_gate_a = d_pool
    off_val = 2 * d_pool
    off_glu = off_val + d_conv
    off_gate_b = off_glu + d_conv

    def pool_row(r, c0, c1):
        if r < n_pst:
            return pst_ref[r, :, c0:c1]
        return z_ref[r - n_pst, :, c0:c1]

    for gi, w in enumerate(POOL_WINDOWS):
        c0, c1 = gi * pg, (gi + 1) * pg
        pooled = []
        for t in range(dec_seq):
            s = pool_row(n_pst + t, c0, c1)
            cur = s
            for k in range(1, w):
                s = s + pool_row(n_pst + t - k, c0, c1)
            cnt = float(min(w, PAST_LEN + t + 1))
            pooled.append((s / cnt - cur).astype(BF16))
        mixed = jnp.dot(jnp.concatenate(pooled, axis=0), wpool_ref[gi], preferred_element_type=F32)
        mixed = mixed * pscale_ref[:, c0:c1]
        for t in range(dec_seq):
            gate = z_ref[t, :, off_gate_a + c0:off_gate_a + c1]
            o_ref[t, :, c0:c1] = (mixed[t * gb:(t + 1) * gb] * gate).astype(BF16)
    for r in range(n_pst):
        pnew_ref[r] = pool_row(r + dec_seq, 0, d_pool)

    cw = 2 * LANES
    ys = [[] for _ in range(dec_seq)]
    for cb in range(d_conv // cw):
        c0, c1 = cb * cw, (cb + 1) * cw
        a_new = [z_ref[t, :, off_val + c0:off_val + c1] * z_ref[t, :, off_glu + c0:off_glu + c1]
                 for t in range(dec_seq)]

        def conv_row(r, c0=c0, c1=c1, a_new=a_new):
            if r < n_cst:
                return cst_ref[r, :, c0:c1]
            return a_new[r - n_cst]

        acc = [jnp.broadcast_to(bdw_ref[:, c0:c1], (gb, cw)) for _ in range(dec_seq)]
        for r in range(n_cst + dec_seq):
            x_r = conv_row(r)
            for t in range(dec_seq):
                k = r - t
                if 0 <= k < CONV_WIDTH:
                    acc[t] = acc[t] + x_r * wdw_ref[k:k + 1, c0:c1]
            if r >= dec_seq:
                cnew_ref[r - dec_seq, :, c0:c1] = x_r
        for t in range(dec_seq):
            ys[t].append(acc[t])
    y = jnp.concatenate([jnp.concatenate(ys[t], axis=1) for t in range(dec_seq)], axis=0)
    act = _silu(_layernorm(y, lng_ref[...], lnb_ref[...]))
    o_b = jnp.dot(act.astype(BF16), wpw_ref[...], preferred_element_type=F32)
    for t in range(dec_seq):
        gate = z_ref[t, :, off_gate_b:off_gate_b + d_conv]
        o_ref[t, :, d_pool:d_pool + d_conv] = (o_b[t * gb:(t + 1) * gb] * gate).astype(BF16)


def _sample_mixer(z3, pstate, cstate, w_pool, pool_scale, w_dw, b_dw, ln_g, ln_b, w_pw, *, gb):
    dec_seq, dec_batch, d_in = z3.shape
    d_pool = pstate.shape[2]
    d_conv = cstate.shape[2]
    assert dec_batch % gb == 0
    kernel = functools.partial(_sample_mixer_kernel, dec_seq=dec_seq, d_pool=d_pool, d_conv=d_conv)
    const2 = lambda i: (0, 0)
    seq_block = lambda a: pl.BlockSpec((a.shape[0], gb, a.shape[2]), lambda i: (0, i, 0))
    return pl.pallas_call(
        kernel,
        grid=(dec_batch // gb,),
        in_specs=[
            seq_block(z3),
            seq_block(pstate),
            seq_block(cstate),
            pl.BlockSpec(w_pool.shape, lambda i: (0, 0, 0)),
            pl.BlockSpec((1, d_pool), const2),
            pl.BlockSpec(w_dw.shape, const2),
            pl.BlockSpec((1, d_conv), const2),
            pl.BlockSpec((1, d_conv), const2),
            pl.BlockSpec((1, d_conv), const2),
            pl.BlockSpec(w_pw.shape, const2),
        ],
        out_specs=[
            pl.BlockSpec((dec_seq, gb, d_pool + d_conv), lambda i: (0, i, 0)),
            seq_block(pstate),
            seq_block(cstate),
        ],
        out_shape=[
            jax.ShapeDtypeStruct((dec_seq, dec_batch, d_pool + d_conv), BF16),
            jax.ShapeDtypeStruct(pstate.shape, F32),
            jax.ShapeDtypeStruct(cstate.shape, F32),
        ],
        compiler_params=_compiler_params(("arbitrary",)),
        name="sample_mixer",
    )(z3, pstate, cstate, w_pool, pool_scale, w_dw, b_dw, ln_g, ln_b, w_pw)


def _sample_attn_kernel(q_ref, gate_ref, k_ref, v_ref, o_ref, *, scale):
    gb, n_rows, _ = k_ref.shape
    n_q = q_ref.shape[1]
    half_rows = n_q // 2
    col = lax.broadcasted_iota(jnp.int32, (n_q, n_rows), 1)
    row = lax.broadcasted_iota(jnp.int32, (n_q, n_rows), 0)
    match = (((col >> 2) & 1) == (row >> 4)) & ((col & 3) == ((row >> 2) & 3))
    valid = match[0:half_rows]
    seqs = range(gb)
    p = [lax.dot_general(q_ref[b].astype(BF16), k_ref[b].astype(BF16), (((1,), (1,)), ((), ())),
                         preferred_element_type=F32) for b in seqs]
    p = [jnp.where(match, p[b], 0.0) for b in seqs]
    s = [p[b][0:half_rows] + pltpu.roll(p[b][half_rows:n_q], n_rows - 4, axis=1) for b in seqs]
    s = [jnp.where(valid, s[b] * scale, -1e30) for b in seqs]
    m = [jnp.max(s[b], axis=-1, keepdims=True) for b in seqs]
    e = [jnp.where(valid, jnp.exp(s[b] - m[b]), 0.0) for b in seqs]
    l = [jnp.sum(e[b], axis=-1, keepdims=True) for b in seqs]
    e2 = [jnp.concatenate([e[b], pltpu.roll(e[b], 4, axis=1)], axis=0).astype(BF16) for b in seqs]
    o = [jnp.dot(e2[b], v_ref[b].astype(BF16), preferred_element_type=F32) for b in seqs]
    for b in seqs:
        o_ref[b] = o[b] / jnp.concatenate([l[b], l[b]], axis=0) * gate_ref[b]


def _sample_attn(qm, gm, cache_k, cache_v, *, gb, xhead_dim):
    nb, n_q, lanes = qm.shape
    n_rows = cache_k.shape[1]
    assert n_q == 2 * N_XHEADS * 4 and N_XHEADS == 4 and lanes == LANES
    kernel = functools.partial(_sample_attn_kernel, scale=xhead_dim ** -0.5)
    small = pl.BlockSpec((gb, n_q, lanes), lambda i: (i, 0, 0))
    big = pl.BlockSpec((gb, n_rows, lanes), lambda i: (i, 0, 0))
    return pl.pallas_call(
        kernel,
        grid=(nb // gb,),
        in_specs=[small, small, big, big],
        out_specs=small,
        out_shape=jax.ShapeDtypeStruct((nb, n_q, lanes), F32),
        compiler_params=_compiler_params(("arbitrary",)),
        name="sample_attn",
    )(qm, gm, cache_k, cache_v)


def _cache_rows(c):
    nb, n_mem, n_heads, e = c.shape
    assert n_heads == N_XHEADS and e == 2 * LANES
    return c.reshape(nb, n_mem, n_heads, 2, LANES).transpose(0, 1, 3, 2, 4).reshape(nb, n_mem * 2 * n_heads, LANES)


def _query_rows(q):
    t, nb, _ = q.shape
    return q.reshape(t, nb, N_XHEADS, 2, LANES).transpose(1, 3, 2, 0, 4).reshape(nb, 2 * N_XHEADS * t, LANES)


def _query_rows_inv(o, t):
    nb = o.shape[0]
    return o.reshape(nb, 2, N_XHEADS, t, LANES).transpose(3, 0, 2, 1, 4).reshape(t, nb, N_XHEADS * 2 * LANES)


def _out_proj_kernel(o_ref, w_ref, x_ref, g_ref, y_hbm, acc_ref, ssq_ref, scale_ref, stage_ref, sem, stage_sem,
                     *, n_i, n_j, tm, tn, d_model):
    i = pl.program_id(0)
    j = pl.program_id(1)
    slot = j % 2

    def dst(jj, row_tile):
        return y_hbm.at[pl.ds(pl.multiple_of(row_tile * tm, tm), tm), jj * tn:(jj + 1) * tn]

    def stage_copy(jj, row_tile):
        return pltpu.make_async_copy(stage_ref.at[jj % 2], dst(jj, row_tile), stage_sem.at[jj % 2])

    def slab_copy(jj, row_tile):
        return pltpu.make_async_copy(acc_ref.at[jj], dst(jj, row_tile), sem.at[jj])

    @pl.when((i == 0) & (j == 0))
    def _():
        acc_ref[...] = jnp.zeros(acc_ref.shape, F32)
        scale_ref[...] = jnp.zeros(scale_ref.shape, F32)

    for jj in range(n_j):
        if jj >= 2:

            @pl.when((j == jj) & (i > 0))
            def _(jj=jj):
                stage_copy(jj - 2, i - 1).wait()
        else:

            @pl.when((j == jj) & (i > 1))
            def _(jj=jj):
                stage_copy(jj + n_j - 2, i - 2).wait()

    stage_ref[slot] = acc_ref[j] * scale_ref[...] * g_ref[j]
    part = x_ref[...] + jnp.dot(o_ref[...], w_ref[...], preferred_element_type=F32)
    acc_ref[j] = part
    row_ssq = jnp.sum(part * part, axis=-1, keepdims=True)

    for jj in range(n_j):

        @pl.when((j == jj) & (i > 0))
        def _(jj=jj):
            stage_copy(jj, i - 1).start()

    @pl.when(j == 0)
    def _():
        ssq_ref[...] = row_ssq

    @pl.when(j > 0)
    def _():
        ssq_ref[...] += row_ssq

    @pl.when(j == n_j - 1)
    def _():
        scale_ref[...] = lax.rsqrt(ssq_ref[...] * (1.0 / d_model) + EPS)

    @pl.when((i == n_i - 1) & (j == n_j - 1))
    def _():
        if n_i > 1:
            stage_copy(n_j - 2, i - 1).wait()
            stage_copy(n_j - 1, i - 1).wait()
        for jj in range(n_j):
            acc_ref[jj] = acc_ref[jj] * scale_ref[...] * g_ref[jj]
            slab_copy(jj, i).start()
        for jj in range(n_j):
            slab_copy(jj, i).wait()


def _out_proj(o, w, x, g, *, tile):
    tm, tn = tile
    m, k = o.shape
    n = w.shape[1]
    n_j = n // tn
    assert m % tm == 0 and n % tn == 0 and n_j % 2 == 0
    n_i = m // tm
    kernel = functools.partial(_out_proj_kernel, n_i=n_i, n_j=n_j, tm=tm, tn=tn, d_model=n)
    g_tiles = g.reshape(n_j, 1, tn)
    return pl.pallas_call(
        kernel,
        grid=(n_i, n_j),
        in_specs=[
            pl.BlockSpec((tm, k), lambda i, j: (i, 0)),
            pl.BlockSpec((k, tn), lambda i, j: (0, j)),
            pl.BlockSpec((tm, tn), lambda i, j: (i, j)),
            pl.BlockSpec((n_j, 1, tn), lambda i, j: (0, 0, 0)),
        ],
        out_specs=pl.BlockSpec(memory_space=pl.ANY),
        out_shape=jax.ShapeDtypeStruct((m, n), F32),
        scratch_shapes=[
            pltpu.VMEM((n_j, tm, tn), F32),
            pltpu.VMEM((tm, 1), F32),
            pltpu.VMEM((tm, 1), F32),
            pltpu.VMEM((2, tm, tn), F32),
            pltpu.SemaphoreType.DMA((n_j,)),
            pltpu.SemaphoreType.DMA((2,)),
        ],
        compiler_params=_compiler_params(("arbitrary", "arbitrary")),
        name="out_proj",
    )(o, w, x, g_tiles)


def kernel(x_prompt, mem_prompt, x_sample, cache_mem_k, cache_mem_v, state_pool, state_conv, norm_g, mem_norm_g, w_in,
           w_mem_k, w_mem_v, w_pool, pool_scale, w_dw, b_dw, conv_ln_g, conv_ln_b, w_pw, w_out, final_norm_g):
    depth = w_in.shape[0]
    assert depth == 1, "single-layer step"
    batch, seq, d_model = x_prompt.shape
    dec_batch, dec_seq, _ = x_sample.shape
    n_mem = mem_prompt.shape[1]
    d_xattn = w_mem_k.shape[2]
    d_pool = pool_scale.shape[1]
    d_conv = w_pw.shape[1]
    xhead_dim = d_xattn // N_XHEADS
    off_q = 2 * d_pool + 3 * d_conv
    l = 0

    g_in = norm_g[l][None, :]
    g_mem = mem_norm_g[l][None, :]
    g_fin = final_norm_g[None, :]
    pscale = pool_scale[l][None, :]
    bdw = b_dw[l][None, :]
    lng = conv_ln_g[l][None, :]
    lnb = conv_ln_b[l][None, :]
    z_segments = ((d_pool, ACT_NONE), (d_pool, ACT_SILU), (d_conv, ACT_NONE), (d_conv, ACT_SIGMOID),
                  (d_conv, ACT_SILU), (d_xattn, ACT_NONE), (d_xattn, ACT_SILU))

    xp = x_prompt.reshape(batch * seq, d_model)
    mem = mem_prompt.reshape(batch * n_mem, d_model)
    k_p, v_p = _norm_matmul_pair(mem, g_mem, w_mem_k[l], w_mem_v[l], tile=MEM_PROJ_TILE)
    k_p = k_p.reshape(batch, n_mem, d_xattn)
    v_p = v_p.reshape(batch, n_mem, d_xattn)
    xs = x_sample.transpose(1, 0, 2).reshape(dec_seq * dec_batch, d_model)
    z_p, z_s, (w_out_b, w_pw_b, w_pool_b) = _stream_norm_matmul(
        xp, xs, g_in, w_in[l], [w_out[l], w_pw[l], w_pool[l].reshape(d_pool, -1)], tile=PROMPT_PROJ_TILE,
        n_chunk=PROMPT_PROJ_NORM_CHUNKS, cast_rows=WEIGHT_CAST_ROWS,
        act_ranges=_act_tile_ranges(z_segments, PROMPT_PROJ_TILE[1]))
    mixer_w = (w_pool_b.reshape(w_pool.shape[1:]), pscale, w_dw[l], bdw, lng, lnb, w_pw_b)
    o_p, pool_p, conv_p = _prompt_mixer(z_p, k_p, v_p, *mixer_w,
                                        batch=batch, seq=seq, tt=PROMPT_MIXER_ROWS, d_model=d_model)
    y_p = _out_proj(o_p, w_out_b, xp, g_fin, tile=OUT_PROJ_TILE).reshape(batch, seq, d_model)

    z_s = z_s.reshape(dec_seq, dec_batch, -1)
    o_ab, pool_s, conv_s = _sample_mixer(z_s, state_pool[l].transpose(1, 0, 2), state_conv[l].transpose(1, 0, 2),
                                         *mixer_w, gb=SAMPLE_MIXER_SEQS)
    o_c = _sample_attn(_query_rows(z_s[:, :, off_q:off_q + d_xattn]), _query_rows(z_s[:, :, off_q + d_xattn:]),
                       _cache_rows(cache_mem_k[l]), _cache_rows(cache_mem_v[l]),
                       gb=SAMPLE_ATTN_SEQS, xhead_dim=xhead_dim)
    o_c = _query_rows_inv(o_c, dec_seq)
    o_s = jnp.concatenate([o_ab, o_c.astype(BF16)], axis=-1).reshape(dec_seq * dec_batch, d_model)
    y_s = _out_proj(o_s, w_out_b, xs, g_fin, tile=SAMPLE_OUT_PROJ_TILE)
    y_s = y_s.reshape(dec_seq, dec_batch, d_model).transpose(1, 0, 2)

    mem_shape = (depth, batch, n_mem, N_XHEADS, xhead_dim)
    return (y_p, y_s, k_p.reshape(mem_shape), v_p.reshape(mem_shape), pool_p[None], conv_p[None],
            pool_s.transpose(1, 0, 2)[None], conv_s.transpose(1, 0, 2)[None])
```

```python
import functools

import jax
import jax.numpy as jnp
from jax import lax
from jax.experimental import pallas as pl
from jax.experimental.pallas import tpu as pltpu

EPS = 1e-6
POOL_WINDOWS = (2, 4, 8, 16)
CONV_WIDTH = 31
N_XHEADS = 4
PAST_LEN = 16384

SUBLANES = 8
LANES = 128
VMEM_LIMIT_BYTES = 56 * 1024 * 1024

PROMPT_PROJ_TILE = (1024, 512)
PROMPT_PROJ_NORM_CHUNKS = 16
W_OUT_CAST_ROWS = 64
MEM_PROJ_TILE = (512, 256)
OUT_PROJ_TILE = (1024, 512)
SAMPLE_OUT_PROJ_TILE = (512, 512)
PROMPT_MIXER_ROWS = 256
POOL_CHUNK = 32
CONV_CHUNK = 64
SAMPLE_MIXER_SEQS = 16
SAMPLE_ATTN_SEQS = 8

BF16 = jnp.bfloat16
F32 = jnp.float32

ACT_NONE, ACT_SILU, ACT_SIGMOID = 0, 1, 2


def _sigmoid(x):
    return 0.5 * jnp.tanh(0.5 * x) + 0.5


def _silu(x):
    return x * _sigmoid(x)


def _activate(r, act):
    if act == ACT_SILU:
        return _silu(r)
    if act == ACT_SIGMOID:
        return _sigmoid(r)
    return r


class _F32View:
    def __init__(self, ref):
        self.ref = ref
        self.shape = ref.shape

    def __getitem__(self, idx):
        return self.ref[idx].astype(F32)


def _rms_scale(x):
    return lax.rsqrt(jnp.mean(x * x, axis=-1, keepdims=True) + EPS)


def _compiler_params(semantics):
    return pltpu.CompilerParams(dimension_semantics=semantics, vmem_limit_bytes=VMEM_LIMIT_BYTES)


def _act_tile_ranges(segments, tn):
    ranges, off = [], 0
    for width, act in segments:
        assert off % tn == 0 and width % tn == 0, "segments must be whole column tiles"
        if act != ACT_NONE:
            ranges.append((off // tn, (off + width) // tn, act))
        off += width
    return tuple(ranges)


def _tile_act(j, act_ranges):
    act = jnp.int32(ACT_NONE)
    for lo, hi, a in act_ranges:
        act = jnp.where((j >= lo) & (j < hi), a, act)
    return act


def _acts_used(act_ranges):
    return sorted({ACT_NONE} | {a for _, _, a in act_ranges})


def _norm_matmul_pair_kernel(x_ref, g_ref, wa_ref, wb_ref, oa_ref, ob_ref, h_ref):
    @pl.when(pl.program_id(1) == 0)
    def _():
        x = x_ref[...]
        h_ref[...] = (x * _rms_scale(x) * g_ref[...]).astype(BF16)

    oa_ref[...] = jnp.dot(h_ref[...], wa_ref[...].astype(BF16), preferred_element_type=F32)
    ob_ref[...] = jnp.dot(h_ref[...], wb_ref[...].astype(BF16), preferred_element_type=F32)


def _stream_norm_matmul_kernel(x_ref, xs_ref, g_ref, w_ref, cast_src_ref, o_ref, os_ref, cast_dst_ref, h_ref, hs_ref,
                               *, n_j, n_chunk, n_chunk_s, chunk, act_ranges):
    s = pl.program_id(0)
    n_pro = n_chunk_s + n_chunk
    step = jnp.maximum(s - n_pro, 0)
    i = step // n_j
    j = step % n_j

    def normalised(src_ref):
        x = src_ref[...]
        return (x * _rms_scale(x) * g_ref[...]).astype(BF16)

    def chunk_rows(c):
        return pl.ds(pl.multiple_of(c * chunk, chunk), chunk)

    @pl.when(s < n_chunk_s)
    def _():
        hs_ref[chunk_rows(s), :] = normalised(xs_ref)

    @pl.when((s >= n_chunk_s) & (s < n_pro))
    def _():
        h_ref[0, chunk_rows(s - n_chunk_s), :] = normalised(x_ref)

    tile_act = _tile_act(j, act_ranges)
    for act in _acts_used(act_ranges):

        @pl.when((s >= n_pro) & (tile_act == act))
        def _(act=act):
            h_ref[(i + 1) % 2, chunk_rows(jnp.clip(j - 1, 0, n_chunk - 1)), :] = normalised(x_ref)
            cast_dst_ref[...] = cast_src_ref[...].astype(BF16)
            w = w_ref[...].astype(BF16)
            o_ref[...] = _activate(jnp.dot(h_ref[i % 2], w, preferred_element_type=F32), act).astype(o_ref.dtype)

            @pl.when(i == 0)
            def _():
                os_ref[...] = _activate(jnp.dot(hs_ref[...], w, preferred_element_type=F32), act).astype(os_ref.dtype)


def _stream_norm_matmul(x, xs, g, w, cast_src, *, tile, n_chunk, cast_rows, act_ranges=()):
    tm, tn = tile
    m, k = x.shape
    ms = xs.shape[0]
    n = w.shape[1]
    n_i, n_j = m // tm, n // tn
    chunk = tm // n_chunk
    n_chunk_s = ms // chunk
    n_pro = n_chunk_s + n_chunk
    n_cast = cast_src.shape[0] // cast_rows
    assert m % tm == 0 and n % tn == 0 and tm % n_chunk == 0 and n_chunk < n_j
    assert ms % chunk == 0 and ms <= tm and cast_src.shape[0] % cast_rows == 0 and n_cast <= n_i * n_j

    def steps(s):
        step = jnp.maximum(s - n_pro, 0)
        return step // n_j, step % n_j

    def x_index(s):
        i, j = steps(s)
        nxt = jnp.minimum((i + 1) * n_chunk + jnp.clip(j - 1, 0, n_chunk - 1), n_i * n_chunk - 1)
        return jnp.where(s < n_pro, jnp.maximum(s - n_chunk_s, 0), nxt), 0

    def os_index(s):
        i, j = steps(s)
        return 0, jnp.where(i == 0, j, n_j - 1)

    cast_index = lambda s: (jnp.minimum(jnp.maximum(s - n_pro, 0), n_cast - 1), 0)
    kernel = functools.partial(_stream_norm_matmul_kernel, n_j=n_j, n_chunk=n_chunk, n_chunk_s=n_chunk_s,
                               chunk=chunk, act_ranges=act_ranges)
    return pl.pallas_call(
        kernel,
        grid=(n_pro + n_i * n_j,),
        in_specs=[
            pl.BlockSpec((chunk, k), x_index),
            pl.BlockSpec((chunk, k), lambda s: (jnp.minimum(s, n_chunk_s - 1), 0)),
            pl.BlockSpec((1, k), lambda s: (0, 0)),
            pl.BlockSpec((k, tn), lambda s: (0, steps(s)[1])),
            pl.BlockSpec((cast_rows, cast_src.shape[1]), cast_index),
        ],
        out_specs=[
            pl.BlockSpec((tm, tn), lambda s: steps(s)),
            pl.BlockSpec((ms, tn), os_index),
            pl.BlockSpec((cast_rows, cast_src.shape[1]), cast_index),
        ],
        out_shape=[
            jax.ShapeDtypeStruct((m, n), BF16),
            jax.ShapeDtypeStruct((ms, n), BF16),
            jax.ShapeDtypeStruct(cast_src.shape, BF16),
        ],
        scratch_shapes=[pltpu.VMEM((2, tm, k), BF16), pltpu.VMEM((ms, k), BF16)],
        compiler_params=_compiler_params(("arbitrary",)),
        name="stream_norm_matmul",
    )(x, xs, g, w, cast_src)


def _norm_matmul_pair(x, g, w_a, w_b, *, tile):
    tm, tn = tile
    m, k = x.shape
    n = w_a.shape[1]
    assert m % tm == 0 and n % tn == 0 and w_b.shape == w_a.shape
    w_spec = pl.BlockSpec((k, tn), lambda i, j: (0, j))
    o_spec = pl.BlockSpec((tm, tn), lambda i, j: (i, j))
    return pl.pallas_call(
        _norm_matmul_pair_kernel,
        grid=(m // tm, n // tn),
        in_specs=[pl.BlockSpec((tm, k), lambda i, j: (i, 0)), pl.BlockSpec((1, k), lambda i, j: (0, 0)), w_spec, w_spec],
        out_specs=[o_spec, o_spec],
        out_shape=[jax.ShapeDtypeStruct((m, n), F32)] * 2,
        scratch_shapes=[pltpu.VMEM((tm, k), BF16)],
        compiler_params=_compiler_params(("arbitrary", "arbitrary")),
        name="norm_matmul_pair",
    )(x, g, w_a, w_b)


POOL_HALO = 16
CONV_HALO = 32


def _pool_tile(extp_ref, pooled_ref, tt, d_pool, t_idx):
    pg = d_pool // len(POOL_WINDOWS)
    row = lax.broadcasted_iota(jnp.int32, (POOL_CHUNK, LANES), 0)

    for base in range(0, tt, POOL_CHUNK):
        n_prev = t_idx * tt + base + row + 1
        for gi, w in enumerate(POOL_WINDOWS):
            inv_cnt = 1.0 / jnp.minimum(w, n_prev).astype(F32)
            for cb in range(gi * pg // LANES, (gi + 1) * pg // LANES):
                cur = extp_ref[cb, base + POOL_HALO:base + POOL_HALO + POOL_CHUNK, :]
                s = cur
                for k in range(1, w):
                    s = s + extp_ref[cb, base + POOL_HALO - k:base + POOL_HALO - k + POOL_CHUNK, :]
                pooled_ref[base:base + POOL_CHUNK, cb * LANES:(cb + 1) * LANES] = s * inv_cnt - cur


def _dwconv_tile(extc_ref, wdw_ref, bdw_ref, y_ref, tt, col_blocks):
    first = CONV_HALO - (CONV_WIDTH - 1)
    for cb in col_blocks:
        c0, c1 = cb * LANES, (cb + 1) * LANES
        for base in range(0, tt, CONV_CHUNK):
            acc = jnp.broadcast_to(bdw_ref[:, c0:c1], (CONV_CHUNK, LANES))
            for k in range(CONV_WIDTH):
                acc = acc + extc_ref[cb, base + first + k:base + first + k + CONV_CHUNK, :] * wdw_ref[k:k + 1, c0:c1]
            y_ref[base:base + CONV_CHUNK, c0:c1] = acc


def _layernorm(y, g, b):
    mu = jnp.mean(y, axis=-1, keepdims=True)
    d = y - mu
    var = jnp.mean(d * d, axis=-1, keepdims=True)
    return d * lax.rsqrt(var + EPS) * g + b


def _attend(q, k, v, xhead_dim):
    s = lax.dot_general(q.astype(BF16), k, (((1,), (1,)), ((), ())), preferred_element_type=F32)
    s = s * (xhead_dim ** -0.5)
    e = jnp.exp(s - jnp.max(s, axis=-1, keepdims=True))
    l = jnp.sum(e, axis=-1, keepdims=True)
    return jnp.dot(e.astype(BF16), v, preferred_element_type=F32) / l


def _prompt_mixer_kernel(z_ref, k_ref, v_ref, wpool_ref, pscale_ref, wdw_ref, bdw_ref, lng_ref, lnb_ref, wpw_ref,
                         o_ref, pstate_ref, cstate_ref, extp_ref, extc_ref, pooled_ref, y_ref,
                         *, tt, d_pool, d_conv, d_xattn):
    z_ref = _F32View(z_ref)
    t_idx = pl.program_id(1)
    n_t = pl.num_programs(1)
    pg = d_pool // len(POOL_WINDOWS)
    xhead_dim = d_xattn // N_XHEADS
    off_gate_a = d_pool
    off_val = 2 * d_pool
    off_glu = off_val + d_conv
    off_gate_b = off_glu + d_conv
    off_q = off_gate_b + d_conv
    off_gate_c = off_q + d_xattn

    @pl.when(t_idx == 0)
    def _():
        extp_ref[:, 0:POOL_HALO, :] = jnp.zeros((d_pool // LANES, POOL_HALO, LANES), F32)
        extc_ref[:, 0:CONV_HALO, :] = jnp.zeros((d_conv // LANES, CONV_HALO, LANES), F32)

    for cb in range(d_pool // LANES):
        extp_ref[cb, POOL_HALO:POOL_HALO + tt, :] = z_ref[:, cb * LANES:(cb + 1) * LANES]
    _pool_tile(extp_ref, pooled_ref, tt, d_pool, t_idx)
    for gi in range(len(POOL_WINDOWS)):
        c0, c1 = gi * pg, (gi + 1) * pg
        mixed = jnp.dot(pooled_ref[:, c0:c1].astype(BF16), wpool_ref[gi], preferred_element_type=F32)
        mixed = mixed * pscale_ref[:, c0:c1]
        o_ref[:, c0:c1] = (mixed * z_ref[:, off_gate_a + c0:off_gate_a + c1]).astype(BF16)

    for cb in range(d_conv // LANES):
        c0, c1 = cb * LANES, (cb + 1) * LANES
        extc_ref[cb, CONV_HALO:CONV_HALO + tt, :] = z_ref[:, off_val + c0:off_val + c1] * z_ref[:, off_glu + c0:off_glu + c1]
    _dwconv_tile(extc_ref, wdw_ref, bdw_ref, y_ref, tt, range(d_conv // LANES))
    act = _silu(_layernorm(y_ref[...], lng_ref[...], lnb_ref[...]))
    o_b = jnp.dot(act.astype(BF16), wpw_ref[...], preferred_element_type=F32)
    o_ref[:, d_pool:d_pool + d_conv] = (o_b * z_ref[:, off_gate_b:off_gate_b + d_conv]).astype(BF16)

    for h in range(N_XHEADS):
        c0, c1 = h * xhead_dim, (h + 1) * xhead_dim
        o_c = _attend(z_ref[:, off_q + c0:off_q + c1], k_ref[0, :, c0:c1].astype(BF16),
                      v_ref[0, :, c0:c1].astype(BF16), xhead_dim)
        o_c = o_c * z_ref[:, off_gate_c + c0:off_gate_c + c1]
        o_ref[:, d_pool + d_conv + c0:d_pool + d_conv + c1] = o_c.astype(BF16)

    @pl.when(t_idx == n_t - 1)
    def _():
        for cb in range(d_pool // LANES):
            pstate_ref[0, :, cb * LANES:(cb + 1) * LANES] = extp_ref[cb, tt + 1:tt + POOL_HALO, :]
        for cb in range(d_conv // LANES):
            cstate_ref[0, :, cb * LANES:(cb + 1) * LANES] = extc_ref[cb, tt + 2:tt + CONV_HALO, :]

    extp_ref[:, 0:POOL_HALO, :] = extp_ref[:, tt:tt + POOL_HALO, :]
    extc_ref[:, 0:CONV_HALO, :] = extc_ref[:, tt:tt + CONV_HALO, :]


def _prompt_mixer(z, k, v, w_pool, pool_scale, w_dw, b_dw, ln_g, ln_b, w_pw, *, batch, seq, tt, d_model):
    d_in = z.shape[1]
    d_pool = w_pool.shape[0] * w_pool.shape[1]
    d_conv = w_pw.shape[0]
    d_xattn = k.shape[2]
    n_mem = k.shape[1]
    n_t = seq // tt
    assert seq % tt == 0 and tt % CONV_CHUNK == 0 and tt % POOL_CHUNK == 0
    kernel = functools.partial(_prompt_mixer_kernel, tt=tt, d_pool=d_pool, d_conv=d_conv, d_xattn=d_xattn)
    const2 = lambda b, t: (0, 0)
    return pl.pallas_call(
        kernel,
        grid=(batch, n_t),
        in_specs=[
            pl.BlockSpec((tt, d_in), lambda b, t: (b * n_t + t, 0)),
            pl.BlockSpec((1, n_mem, d_xattn), lambda b, t: (b, 0, 0)),
            pl.BlockSpec((1, n_mem, d_xattn), lambda b, t: (b, 0, 0)),
            pl.BlockSpec(w_pool.shape, lambda b, t: (0, 0, 0)),
            pl.BlockSpec((1, d_pool), const2),
            pl.BlockSpec(w_dw.shape, const2),
            pl.BlockSpec((1, d_conv), const2),
            pl.BlockSpec((1, d_conv), const2),
            pl.BlockSpec((1, d_conv), const2),
            pl.BlockSpec(w_pw.shape, const2),
        ],
        out_specs=[
            pl.BlockSpec((tt, d_model), lambda b, t: (b * n_t + t, 0)),
            pl.BlockSpec((1, POOL_HALO - 1, d_pool), lambda b, t: (b, 0, 0)),
            pl.BlockSpec((1, CONV_HALO - 2, d_conv), lambda b, t: (b, 0, 0)),
        ],
        out_shape=[
            jax.ShapeDtypeStruct((batch * seq, d_model), BF16),
            jax.ShapeDtypeStruct((batch, POOL_HALO - 1, d_pool), F32),
            jax.ShapeDtypeStruct((batch, CONV_HALO - 2, d_conv), F32),
        ],
        scratch_shapes=[
            pltpu.VMEM((d_pool // LANES, POOL_HALO + tt, LANES), F32),
            pltpu.VMEM((d_conv // LANES, CONV_HALO + tt, LANES), F32),
            pltpu.VMEM((tt, d_pool), F32),
            pltpu.VMEM((tt, d_conv), F32),
        ],
        compiler_params=_compiler_params(("arbitrary", "arbitrary")),
        name="prompt_mixer",
    )(z, k, v, w_pool, pool_scale, w_dw, b_dw, ln_g, ln_b, w_pw)


def _sample_mixer_kernel(z_ref, pst_ref, cst_ref, wpool_ref, pscale_ref, wdw_ref, bdw_ref, lng_ref, lnb_ref, wpw_ref,
                         o_ref, pnew_ref, cnew_ref, *, dec_seq, d_pool, d_conv):
    z_ref = _F32View(z_ref)
    pg = d_pool // len(POOL_WINDOWS)
    n_pst = pst_ref.shape[0]
    n_cst = cst_ref.shape[0]
    gb = z_ref.shape[1]
    off_gate_a = d_pool
    off_val = 2 * d_pool
    off_glu = off_val + d_conv
    off_gate_b = off_glu + d_conv

    def pool_row(r, c0, c1):
        if r < n_pst:
            return pst_ref[r, :, c0:c1]
        return z_ref[r - n_pst, :, c0:c1]

    for gi, w in enumerate(POOL_WINDOWS):
        c0, c1 = gi * pg, (gi + 1) * pg
        pooled = []
        for t in range(dec_seq):
            s = pool_row(n_pst + t, c0, c1)
            cur = s
            for k in range(1, w):
                s = s + pool_row(n_pst + t - k, c0, c1)
            cnt = float(min(w, PAST_LEN + t + 1))
            pooled.append((s / cnt - cur).astype(BF16))
        mixed = jnp.dot(jnp.concatenate(pooled, axis=0), wpool_ref[gi], preferred_element_type=F32)
        mixed = mixed * pscale_ref[:, c0:c1]
        for t in range(dec_seq):
            gate = z_ref[t, :, off_gate_a + c0:off_gate_a + c1]
            o_ref[t, :, c0:c1] = (mixed[t * gb:(t + 1) * gb] * gate).astype(BF16)
    for r in range(n_pst):
        pnew_ref[r] = pool_row(r + dec_seq, 0, d_pool)

    cw = 2 * LANES
    ys = [[] for _ in range(dec_seq)]
    for cb in range(d_conv // cw):
        c0, c1 = cb * cw, (cb + 1) * cw
        a_new = [z_ref[t, :, off_val + c0:off_val + c1] * z_ref[t, :, off_glu + c0:off_glu + c1]
                 for t in range(dec_seq)]

        def conv_row(r, c0=c0, c1=c1, a_new=a_new):
            if r < n_cst:
                return cst_ref[r, :, c0:c1]
            return a_new[r - n_cst]

        acc = [jnp.broadcast_to(bdw_ref[:, c0:c1], (gb, cw)) for _ in range(dec_seq)]
        for r in range(n_cst + dec_seq):
            x_r = conv_row(r)
            for t in range(dec_seq):
                k = r - t
                if 0 <= k < CONV_WIDTH:
                    acc[t] = acc[t] + x_r * wdw_ref[k:k + 1, c0:c1]
            if r >= dec_seq:
                cnew_ref[r - dec_seq, :, c0:c1] = x_r
        for t in range(dec_seq):
            ys[t].append(acc[t])
    y = jnp.concatenate([jnp.concatenate(ys[t], axis=1) for t in range(dec_seq)], axis=0)
    act = _silu(_layernorm(y, lng_ref[...], lnb_ref[...]))
    o_b = jnp.dot(act.astype(BF16), wpw_ref[...], preferred_element_type=F32)
    for t in range(dec_seq):
        gate = z_ref[t, :, off_gate_b:off_gate_b + d_conv]
        o_ref[t, :, d_pool:d_pool + d_conv] = (o_b[t * gb:(t + 1) * gb] * gate).astype(BF16)


def _sample_mixer(z3, pstate, cstate, w_pool, pool_scale, w_dw, b_dw, ln_g, ln_b, w_pw, *, gb):
    dec_seq, dec_batch, d_in = z3.shape
    d_pool = pstate.shape[2]
    d_conv = cstate.shape[2]
    assert dec_batch % gb == 0
    kernel = functools.partial(_sample_mixer_kernel, dec_seq=dec_seq, d_pool=d_pool, d_conv=d_conv)
    const2 = lambda i: (0, 0)
    seq_block = lambda a: pl.BlockSpec((a.shape[0], gb, a.shape[2]), lambda i: (0, i, 0))
    return pl.pallas_call(
        kernel,
        grid=(dec_batch // gb,),
        in_specs=[
            seq_block(z3),
            seq_block(pstate),
            seq_block(cstate),
            pl.BlockSpec(w_pool.shape, lambda i: (0, 0, 0)),
            pl.BlockSpec((1, d_pool), const2),
            pl.BlockSpec(w_dw.shape, const2),
            pl.BlockSpec((1, d_conv), const2),
            pl.BlockSpec((1, d_conv), const2),
            pl.BlockSpec((1, d_conv), const2),
            pl.BlockSpec(w_pw.shape, const2),
        ],
        out_specs=[
            pl.BlockSpec((dec_seq, gb, d_pool + d_conv), lambda i: (0, i, 0)),
            seq_block(pstate),
            seq_block(cstate),
        ],
        out_shape=[
            jax.ShapeDtypeStruct((dec_seq, dec_batch, d_pool + d_conv), BF16),
            jax.ShapeDtypeStruct(pstate.shape, F32),
            jax.ShapeDtypeStruct(cstate.shape, F32),
        ],
        compiler_params=_compiler_params(("arbitrary",)),
        name="sample_mixer",
    )(z3, pstate, cstate, w_pool, pool_scale, w_dw, b_dw, ln_g, ln_b, w_pw)


def _sample_attn_kernel(q_ref, gate_ref, k_ref, v_ref, o_ref, *, scale):
    gb, n_rows, _ = k_ref.shape
    n_q = q_ref.shape[1]
    half_rows = n_q // 2
    col = lax.broadcasted_iota(jnp.int32, (n_q, n_rows), 1)
    row = lax.broadcasted_iota(jnp.int32, (n_q, n_rows), 0)
    match = (((col >> 2) & 1) == (row >> 4)) & ((col & 3) == ((row >> 2) & 3))
    valid = match[0:half_rows]
    seqs = range(gb)
    p = [lax.dot_general(q_ref[b].astype(BF16), k_ref[b].astype(BF16), (((1,), (1,)), ((), ())),
                         preferred_element_type=F32) for b in seqs]
    p = [jnp.where(match, p[b], 0.0) for b in seqs]
    s = [p[b][0:half_rows] + pltpu.roll(p[b][half_rows:n_q], n_rows - 4, axis=1) for b in seqs]
    s = [jnp.where(valid, s[b] * scale, -1e30) for b in seqs]
    m = [jnp.max(s[b], axis=-1, keepdims=True) for b in seqs]
    e = [jnp.where(valid, jnp.exp(s[b] - m[b]), 0.0) for b in seqs]
    l = [jnp.sum(e[b], axis=-1, keepdims=True) for b in seqs]
    e2 = [jnp.concatenate([e[b], pltpu.roll(e[b], 4, axis=1)], axis=0).astype(BF16) for b in seqs]
    o = [jnp.dot(e2[b], v_ref[b].astype(BF16), preferred_element_type=F32) for b in seqs]
    for b in seqs:
        o_ref[b] = o[b] / jnp.concatenate([l[b], l[b]], axis=0) * gate_ref[b]


def _sample_attn(qm, gm, cache_k, cache_v, *, gb, xhead_dim):
    nb, n_q, lanes = qm.shape
    n_rows = cache_k.shape[1]
    assert n_q == 2 * N_XHEADS * 4 and N_XHEADS == 4 and lanes == LANES
    kernel = functools.partial(_sample_attn_kernel, scale=xhead_dim ** -0.5)
    small = pl.BlockSpec((gb, n_q, lanes), lambda i: (i, 0, 0))
    big = pl.BlockSpec((gb, n_rows, lanes), lambda i: (i, 0, 0))
    return pl.pallas_call(
        kernel,
        grid=(nb // gb,),
        in_specs=[small, small, big, big],
        out_specs=small,
        out_shape=jax.ShapeDtypeStruct((nb, n_q, lanes), F32),
        compiler_params=_compiler_params(("arbitrary",)),
        name="sample_attn",
    )(qm, gm, cache_k, cache_v)


def _cache_rows(c):
    nb, n_mem, n_heads, e = c.shape
    assert n_heads == N_XHEADS and e == 2 * LANES
    return c.reshape(nb, n_mem, n_heads, 2, LANES).transpose(0, 1, 3, 2, 4).reshape(nb, n_mem * 2 * n_heads, LANES)


def _query_rows(q):
    t, nb, _ = q.shape
    return q.reshape(t, nb, N_XHEADS, 2, LANES).transpose(1, 3, 2, 0, 4).reshape(nb, 2 * N_XHEADS * t, LANES)


def _query_rows_inv(o, t):
    nb = o.shape[0]
    return o.reshape(nb, 2, N_XHEADS, t, LANES).transpose(3, 0, 2, 1, 4).reshape(t, nb, N_XHEADS * 2 * LANES)


def _out_proj_kernel(o_ref, w_ref, x_ref, g_ref, y_hbm, acc_ref, ssq_ref, scale_ref, stage_ref, sem, stage_sem,
                     *, n_i, n_j, tm, tn, d_model):
    i = pl.program_id(0)
    j = pl.program_id(1)
    slot = j % 2

    def dst(jj, row_tile):
        return y_hbm.at[pl.ds(pl.multiple_of(row_tile * tm, tm), tm), jj * tn:(jj + 1) * tn]

    def stage_copy(jj, row_tile):
        return pltpu.make_async_copy(stage_ref.at[jj % 2], dst(jj, row_tile), stage_sem.at[jj % 2])

    def slab_copy(jj, row_tile):
        return pltpu.make_async_copy(acc_ref.at[jj], dst(jj, row_tile), sem.at[jj])

    @pl.when((i == 0) & (j == 0))
    def _():
        acc_ref[...] = jnp.zeros(acc_ref.shape, F32)
        scale_ref[...] = jnp.zeros(scale_ref.shape, F32)

    for jj in range(n_j):
        if jj >= 2:

            @pl.when((j == jj) & (i > 0))
            def _(jj=jj):
                stage_copy(jj - 2, i - 1).wait()
        else:

            @pl.when((j == jj) & (i > 1))
            def _(jj=jj):
                stage_copy(jj + n_j - 2, i - 2).wait()

    stage_ref[slot] = acc_ref[j] * scale_ref[...] * g_ref[j]
    part = x_ref[...] + jnp.dot(o_ref[...], w_ref[...], preferred_element_type=F32)
    acc_ref[j] = part
    row_ssq = jnp.sum(part * part, axis=-1, keepdims=True)

    for jj in range(n_j):

        @pl.when((j == jj) & (i > 0))
        def _(jj=jj):
            stage_copy(jj, i - 1).start()

    @pl.when(j == 0)
    def _():
        ssq_ref[...] = row_ssq

    @pl.when(j > 0)
    def _():
        ssq_ref[...] += row_ssq

    @pl.when(j == n_j - 1)
    def _():
        scale_ref[...] = lax.rsqrt(ssq_ref[...] * (1.0 / d_model) + EPS)

    @pl.when((i == n_i - 1) & (j == n_j - 1))
    def _():
        if n_i > 1:
            stage_copy(n_j - 2, i - 1).wait()
            stage_copy(n_j - 1, i - 1).wait()
        for jj in range(n_j):
            acc_ref[jj] = acc_ref[jj] * scale_ref[...] * g_ref[jj]
            slab_copy(jj, i).start()
        for jj in range(n_j):
            slab_copy(jj, i).wait()


def _out_proj(o, w, x, g, *, tile):
    tm, tn = tile
    m, k = o.shape
    n = w.shape[1]
    n_j = n // tn
    assert m % tm == 0 and n % tn == 0 and n_j % 2 == 0
    n_i = m // tm
    kernel = functools.partial(_out_proj_kernel, n_i=n_i, n_j=n_j, tm=tm, tn=tn, d_model=n)
    g_tiles = g.reshape(n_j, 1, tn)
    return pl.pallas_call(
        kernel,
        grid=(n_i, n_j),
        in_specs=[
            pl.BlockSpec((tm, k), lambda i, j: (i, 0)),
            pl.BlockSpec((k, tn), lambda i, j: (0, j)),
            pl.BlockSpec((tm, tn), lambda i, j: (i, j)),
            pl.BlockSpec((n_j, 1, tn), lambda i, j: (0, 0, 0)),
        ],
        out_specs=pl.BlockSpec(memory_space=pl.ANY),
        out_shape=jax.ShapeDtypeStruct((m, n), F32),
        scratch_shapes=[
            pltpu.VMEM((n_j, tm, tn), F32),
            pltpu.VMEM((tm, 1), F32),
            pltpu.VMEM((tm, 1), F32),
            pltpu.VMEM((2, tm, tn), F32),
            pltpu.SemaphoreType.DMA((n_j,)),
            pltpu.SemaphoreType.DMA((2,)),
        ],
        compiler_params=_compiler_params(("arbitrary", "arbitrary")),
        name="out_proj",
    )(o, w, x, g_tiles)


def kernel(x_prompt, mem_prompt, x_sample, cache_mem_k, cache_mem_v, state_pool, state_conv, norm_g, mem_norm_g, w_in,
           w_mem_k, w_mem_v, w_pool, pool_scale, w_dw, b_dw, conv_ln_g, conv_ln_b, w_pw, w_out, final_norm_g):
    depth = w_in.shape[0]
    assert depth == 1, "single-layer step"
    batch, seq, d_model = x_prompt.shape
    dec_batch, dec_seq, _ = x_sample.shape
    n_mem = mem_prompt.shape[1]
    d_xattn = w_mem_k.shape[2]
    d_pool = pool_scale.shape[1]
    d_conv = w_pw.shape[1]
    xhead_dim = d_xattn // N_XHEADS
    off_q = 2 * d_pool + 3 * d_conv
    l = 0

    w_pool_b = w_pool[l].astype(BF16)
    w_pw_b = w_pw[l].astype(BF16)
    g_in = norm_g[l][None, :]
    g_mem = mem_norm_g[l][None, :]
    g_fin = final_norm_g[None, :]
    pscale = pool_scale[l][None, :]
    bdw = b_dw[l][None, :]
    lng = conv_ln_g[l][None, :]
    lnb = conv_ln_b[l][None, :]
    mixer_w = (w_pool_b, pscale, w_dw[l], bdw, lng, lnb, w_pw_b)
    z_segments = ((d_pool, ACT_NONE), (d_pool, ACT_SILU), (d_conv, ACT_NONE), (d_conv, ACT_SIGMOID),
                  (d_conv, ACT_SILU), (d_xattn, ACT_NONE), (d_xattn, ACT_SILU))

    xp = x_prompt.reshape(batch * seq, d_model)
    mem = mem_prompt.reshape(batch * n_mem, d_model)
    k_p, v_p = _norm_matmul_pair(mem, g_mem, w_mem_k[l], w_mem_v[l], tile=MEM_PROJ_TILE)
    k_p = k_p.reshape(batch, n_mem, d_xattn)
    v_p = v_p.reshape(batch, n_mem, d_xattn)
    xs = x_sample.transpose(1, 0, 2).reshape(dec_seq * dec_batch, d_model)
    z_p, z_s, w_out_b = _stream_norm_matmul(
        xp, xs, g_in, w_in[l], w_out[l], tile=PROMPT_PROJ_TILE, n_chunk=PROMPT_PROJ_NORM_CHUNKS,
        cast_rows=W_OUT_CAST_ROWS, act_ranges=_act_tile_ranges(z_segments, PROMPT_PROJ_TILE[1]))
    o_p, pool_p, conv_p = _prompt_mixer(z_p, k_p, v_p, *mixer_w,
                                        batch=batch, seq=seq, tt=PROMPT_MIXER_ROWS, d_model=d_model)
    y_p = _out_proj(o_p, w_out_b, xp, g_fin, tile=OUT_PROJ_TILE).reshape(batch, seq, d_model)

    z_s = z_s.reshape(dec_seq, dec_batch, -1)
    o_ab, pool_s, conv_s = _sample_mixer(z_s, state_pool[l].transpose(1, 0, 2), state_conv[l].transpose(1, 0, 2),
                                         *mixer_w, gb=SAMPLE_MIXER_SEQS)
    o_c = _sample_attn(_query_rows(z_s[:, :, off_q:off_q + d_xattn]), _query_rows(z_s[:, :, off_q + d_xattn:]),
                       _cache_rows(cache_mem_k[l]), _cache_rows(cache_mem_v[l]),
                       gb=SAMPLE_ATTN_SEQS, xhead_dim=xhead_dim)
    o_c = _query_rows_inv(o_c, dec_seq)
    o_s = jnp.concatenate([o_ab, o_c.astype(BF16)], axis=-1).reshape(dec_seq * dec_batch, d_model)
    y_s = _out_proj(o_s, w_out_b, xs, g_fin, tile=SAMPLE_OUT_PROJ_TILE)
    y_s = y_s.reshape(dec_seq, dec_batch, d_model).transpose(1, 0, 2)

    mem_shape = (depth, batch, n_mem, N_XHEADS, xhead_dim)
    return (y_p, y_s, k_p.reshape(mem_shape), v_p.reshape(mem_shape), pool_p[None], conv_p[None],
            pool_s.transpose(1, 0, 2)[None], conv_s.transpose(1, 0, 2)[None])
```
